```python
import jax, jax.numpy as jnp
from jax import lax
import numpy as np

D_MODEL = 1024
BATCH = 16
SEQ = 2048
DEPTH = 1

ATTN_HEADS = 16
ATTN_KV_HEADS = 4
ATTN_HEAD_DIM = 64
ATTN_WIDTH = ATTN_HEADS * ATTN_HEAD_DIM
KV_WIDTH = ATTN_KV_HEADS * ATTN_HEAD_DIM
WINDOW = 128
ATTN_BLOCK = 128
ROPE_THETA = 10000.0
REC_HEADS = 8
REC_KEY_DIM = 128
REC_VAL_DIM = 128
REC_KEY_WIDTH = REC_HEADS * REC_KEY_DIM
REC_WIDTH = REC_HEADS * REC_VAL_DIM
REC_CHUNK = 64
EPS = 1e-6
D_MIX = ATTN_WIDTH + REC_WIDTH

IN_SPLITS = (ATTN_WIDTH, KV_WIDTH, KV_WIDTH, ATTN_WIDTH,
             REC_KEY_WIDTH, REC_KEY_WIDTH, REC_WIDTH, REC_WIDTH)
IN_WIDTH = sum(IN_SPLITS)

kernel_name = "hymba_swa_sink_hgrn2_adaln"


def rms_norm(x, w):
    xf = x.astype(jnp.float32)
    y = xf * lax.rsqrt(jnp.mean(xf * xf, axis=-1, keepdims=True) + EPS)
    return (y * w.astype(jnp.float32)).astype(x.dtype)


def rope(x, positions):
    hd = x.shape[-1]
    inv_freq = ROPE_THETA ** (-jnp.arange(0, hd, 2, dtype=jnp.float32) / hd)
    ang = positions.astype(jnp.float32)[..., None] * inv_freq
    cos = jnp.cos(ang)[:, :, None, :]
    sin = jnp.sin(ang)[:, :, None, :]
    xf = x.astype(jnp.float32)
    x1, x2 = jnp.split(xf, 2, axis=-1)
    out = jnp.concatenate([x1 * cos - x2 * sin, x2 * cos + x1 * sin], axis=-1)
    return out.astype(x.dtype)


def sliding_window_attention(q, k, v, sinks):
    B, S, Hq, hd = q.shape
    Hkv = k.shape[2]
    G = Hq // Hkv
    nb = S // ATTN_BLOCK
    qb = q.reshape(B, nb, ATTN_BLOCK, Hkv, G, hd)
    kb = k.reshape(B, nb, ATTN_BLOCK, Hkv, hd)
    vb = v.reshape(B, nb, ATTN_BLOCK, Hkv, hd)

    def with_prev(t):
        prev = jnp.pad(t[:, :-1], ((0, 0), (1, 0), (0, 0), (0, 0), (0, 0)))
        return jnp.concatenate([prev, t], axis=2)

    kk, vv = with_prev(kb), with_prev(vb)
    s = jnp.einsum('bnqhgd,bnkhd->bnhgqk', qb, kk,
                   preferred_element_type=jnp.float32) * (hd ** -0.5)
    qi = jnp.arange(ATTN_BLOCK)[:, None]
    kj = jnp.arange(2 * ATTN_BLOCK)[None, :]
    dist = ATTN_BLOCK + qi - kj
    band = (dist >= 0) & (dist < WINDOW)
    blk = jnp.arange(nb)[:, None, None]
    mask = band[None] & ((blk > 0) | (kj >= ATTN_BLOCK)[None])
    s = jnp.where(mask[None, :, None, None], s, -jnp.inf)
    sink = sinks.astype(jnp.float32).reshape(Hkv, G)[None, None, :, :, None, None]
    m = jnp.maximum(jnp.max(s, axis=-1, keepdims=True), sink)
    p = jnp.exp(s - m)
    denom = jnp.sum(p, axis=-1, keepdims=True) + jnp.exp(sink - m)
    o = jnp.einsum('bnhgqk,bnkhd->bnqhgd', (p / denom).astype(v.dtype), vv)
    return o.reshape(B, S, Hq * hd)


def hgrn2_chunkwise(q, f, i):
    B, S, H, dk = q.shape
    dv = i.shape[-1]
    C = REC_CHUNK
    nc = S // C
    k = (1.0 - f).reshape(B, nc, C, H, dk)
    logf = jnp.log(f).reshape(B, nc, C, H, dk)
    q = q.reshape(B, nc, C, H, dk)
    v = i.reshape(B, nc, C, H, dv)
    b = jnp.cumsum(logf, axis=2)
    b_last = b[:, :, -1]
    q_dec = q * jnp.exp(b)
    k_inv = k * jnp.exp(-b)
    k_out = k * jnp.exp(b_last[:, :, None] - b)
    causal = jnp.tril(jnp.ones((C, C), dtype=bool))
    a = jnp.einsum('bnthk,bnshk->bnhts', q_dec, k_inv)
    a = jnp.where(causal, a, 0.0)
    o_intra = jnp.einsum('bnhts,bnshv->bnthv', a, v)
    u = jnp.einsum('bnshk,bnshv->bnhkv', k_out, v)
    decay = jnp.exp(b_last)

    def step(state, xs):
        d, u_c = xs
        return d[..., None] * state + u_c, state

    s0 = jnp.zeros((B, H, dk, dv), jnp.float32)
    _, s_before = lax.scan(step, s0, (jnp.moveaxis(decay, 1, 0), jnp.moveaxis(u, 1, 0)))
    s_before = jnp.moveaxis(s_before, 0, 1)
    o_inter = jnp.einsum('bnthk,bnhkv->bnthv', q_dec, s_before)
    return (o_intra + o_inter).reshape(B, S, H, dv)


def hybrid_layer(x, c_act, positions, lb, norm_w, w_ada, b_ada, w_in,
                 q_norm_w, k_norm_w, sinks, rec_norm_w, w_out):
    B, S, _ = x.shape
    mod = c_act @ w_ada + b_ada
    shift, scale, gate = jnp.split(mod, 3, axis=-1)
    h = rms_norm(x, norm_w) * (1.0 + scale[:, None]) + shift[:, None]
    proj = h @ w_in
    offsets = [int(o) for o in np.cumsum(IN_SPLITS)[:-1]]
    aq, ak, av, ag, rq, rf, ri, rg = jnp.split(proj, offsets, axis=-1)

    aq = rope(rms_norm(aq.reshape(B, S, ATTN_HEADS, ATTN_HEAD_DIM), q_norm_w), positions)
    ak = rope(rms_norm(ak.reshape(B, S, ATTN_KV_HEADS, ATTN_HEAD_DIM), k_norm_w), positions)
    av = av.reshape(B, S, ATTN_KV_HEADS, ATTN_HEAD_DIM)
    attn = sliding_window_attention(aq, ak, av, sinks) * jax.nn.silu(ag)

    f = lb + (1.0 - lb) * jax.nn.sigmoid(rf.astype(jnp.float32))
    rec = hgrn2_chunkwise(rq.astype(jnp.float32).reshape(B, S, REC_HEADS, REC_KEY_DIM),
                          f.reshape(B, S, REC_HEADS, REC_KEY_DIM),
                          ri.astype(jnp.float32).reshape(B, S, REC_HEADS, REC_VAL_DIM))
    rec = rms_norm(rec, rec_norm_w).reshape(B, S, REC_WIDTH).astype(x.dtype)
    rec = rec * jax.nn.silu(rg)

    mixed = jnp.concatenate([attn, rec], axis=-1) @ w_out
    return x + gate[:, None] * mixed


def setup_inputs(seed: int = 0) -> dict:
    key = jax.random.key(seed)
    ks = jax.random.split(key, 16)
    f32 = jnp.float32
    x = jax.random.normal(ks[0], (BATCH, SEQ, D_MODEL), f32)
    c = jax.random.normal(ks[1], (BATCH, D_MODEL), f32)
    positions = jnp.broadcast_to(jnp.arange(SEQ, dtype=jnp.int32), (BATCH, SEQ))
    norm_w = 1.0 + 0.02 * jax.random.normal(ks[2], (DEPTH, D_MODEL), f32)
    w_ada = jax.random.normal(ks[3], (DEPTH, D_MODEL, 3 * D_MODEL), f32) * D_MODEL ** -0.5
    b_ada = 0.02 * jax.random.normal(ks[4], (DEPTH, 3 * D_MODEL), f32)
    w_in = jax.random.normal(ks[5], (DEPTH, D_MODEL, IN_WIDTH), f32) * D_MODEL ** -0.5
    q_norm_w = 1.0 + 0.02 * jax.random.normal(ks[6], (DEPTH, ATTN_HEAD_DIM), f32)
    k_norm_w = 1.0 + 0.02 * jax.random.normal(ks[7], (DEPTH, ATTN_HEAD_DIM), f32)
    sinks = 0.5 * jax.random.normal(ks[8], (DEPTH, ATTN_HEADS), f32)
    rec_norm_w = 1.0 + 0.02 * jax.random.normal(ks[9], (DEPTH, REC_VAL_DIM), f32)
    lower_bounds = 0.1 * jax.random.normal(ks[10], (DEPTH + 1, REC_KEY_WIDTH), f32)
    w_out = jax.random.normal(ks[11], (DEPTH, D_MIX, D_MODEL), f32) * D_MIX ** -0.5
    return {"x": x, "c": c, "positions": positions, "norm_w": norm_w,
            "w_ada": w_ada, "b_ada": b_ada, "w_in": w_in, "q_norm_w": q_norm_w,
            "k_norm_w": k_norm_w, "sinks": sinks, "rec_norm_w": rec_norm_w,
            "lower_bounds": lower_bounds, "w_out": w_out}


def reference(x, c, positions, norm_w, w_ada, b_ada, w_in, q_norm_w, k_norm_w,
              sinks, rec_norm_w, lower_bounds, w_out):
    c_act = jax.nn.silu(c)
    lb_all = jnp.cumsum(jax.nn.softmax(lower_bounds.astype(jnp.float32), axis=0), axis=0)
    h = x
    for l in range(DEPTH):
        h = hybrid_layer(h, c_act, positions, lb_all[l], norm_w[l], w_ada[l], b_ada[l],
                         w_in[l], q_norm_w[l], k_norm_w[l], sinks[l], rec_norm_w[l],
                         w_out[l])
    return h
```

```python
import functools

import jax
import jax.numpy as jnp
import numpy as np
from jax import lax
from jax.experimental import pallas as pl
from jax.experimental.pallas import tpu as pltpu

F32 = jnp.float32
BF16 = jnp.bfloat16

D_MODEL = 1024
ATTN_HEADS = 16
ATTN_KV_HEADS = 4
HEAD_DIM = 64
KV_WIDTH = ATTN_KV_HEADS * HEAD_DIM
ATTN_WIDTH = ATTN_HEADS * HEAD_DIM
WINDOW = 128
ROPE_THETA = 10000.0
REC_HEADS = 8
REC_DIM = 128
REC_WIDTH = REC_HEADS * REC_DIM
EPS = 1e-6
BLK = 128
LANES = 128

OFF_AQ = 0
OFF_AK = OFF_AQ + ATTN_WIDTH
OFF_AV = OFF_AK + KV_WIDTH
OFF_AG = OFF_AV + KV_WIDTH
OFF_RQ = OFF_AG + ATTN_WIDTH
OFF_RF = OFF_RQ + REC_WIDTH
OFF_RI = OFF_RF + REC_WIDTH
OFF_RG = OFF_RI + REC_WIDTH
IN_WIDTH = OFF_RG + REC_WIDTH

TOKEN_TILE = 512
VMEM_LIMIT_BYTES = 56 * 1024 * 1024

_NT = (((1,), (1,)), ((), ()))


def _sigmoid(x):
    return 1.0 / (1.0 + jnp.exp(-x))


def _ada_kernel(c_ref, w_ref, b_ref, lbin_ref, mod_ref, lb_ref):
    c = c_ref[...]
    c_act = c * _sigmoid(c)
    mod_ref[...] = jnp.dot(c_act.astype(BF16), w_ref[...].astype(BF16),
                           preferred_element_type=F32) + b_ref[...]
    lbv = lbin_ref[...]
    e = jnp.exp(lbv - jnp.max(lbv, axis=0, keepdims=True))
    lb_ref[...] = e[0:1, :] / jnp.sum(e, axis=0, keepdims=True)


def _layer_kernel(sinks_ref, x_ref, pos_ref, mod_ref, normw_ref, win_ref, wout_ref,
                  qw_ref, kw_ref, recw_ref, lb_ref, invf_ref, sgn_ref, bd_ref, mrel_ref,
                  bias_ref, o_ref,
                  hb_s, cos_s, sin_s, q_s, klo_s, khi_s, vlo_s, vhi_s, ga_s,
                  rq_s, rf_s, vr_s, gr_s, st_s, mix_s):
    T = x_ref.shape[1]
    nblk = T // BLK
    t = pl.program_id(1)

    x = x_ref[0]
    mod = mod_ref[0]
    shift = mod[:, 0:D_MODEL]
    scale = mod[:, D_MODEL:2 * D_MODEL]
    ms = jnp.mean(x * x, axis=-1, keepdims=True)
    h = (x * lax.rsqrt(ms + EPS)) * (normw_ref[...] * (1.0 + scale)) + shift
    hb_s[...] = h.astype(BF16)

    ang = pos_ref[0].astype(F32) * invf_ref[...]
    cos_s[...] = jnp.cos(ang)
    sin_s[...] = jnp.sin(ang) * sgn_ref[...]

    lane = lax.broadcasted_iota(jnp.int32, (T, LANES), 1)
    first_half = (lane & (HEAD_DIM // 2)) == 0
    low_head = lane < HEAD_DIM

    def qk_norm(raw, w_row):
        ssq = jnp.dot((raw * raw).astype(BF16), bd_ref[...], preferred_element_type=F32)
        return raw * lax.rsqrt(ssq * (1.0 / HEAD_DIM) + EPS) * w_row

    def rope(v):
        partner = jnp.where(first_half, pltpu.roll(v, LANES - HEAD_DIM // 2, 1),
                            pltpu.roll(v, HEAD_DIM // 2, 1))
        return v * cos_s[...] + partner * sin_s[...]

    hb = hb_s[...]

    for cb in range(ATTN_WIDTH // 256):
        c0 = OFF_AQ + cb * 256
        qn = qk_norm(jnp.dot(hb, win_ref[:, c0:c0 + 256], preferred_element_type=F32), qw_ref[...])
        for half in range(2):
            q_s[:, cb * 256 + half * LANES:cb * 256 + (half + 1) * LANES] = rope(
                qn[:, half * LANES:(half + 1) * LANES]).astype(BF16)

    def store_split(lo_s, hi_s, pair, kv_pair):
        sw = pltpu.roll(pair, HEAD_DIM, 1)
        zero = jnp.zeros_like(pair)
        a, b = 2 * kv_pair, 2 * kv_pair + 1
        lo_s[BLK:BLK + T, a * LANES:(a + 1) * LANES] = jnp.where(low_head, pair, zero).astype(BF16)
        hi_s[BLK:BLK + T, a * LANES:(a + 1) * LANES] = jnp.where(low_head, zero, sw).astype(BF16)
        lo_s[BLK:BLK + T, b * LANES:(b + 1) * LANES] = jnp.where(low_head, sw, zero).astype(BF16)
        hi_s[BLK:BLK + T, b * LANES:(b + 1) * LANES] = jnp.where(low_head, zero, pair).astype(BF16)

    @pl.when(t == 0)
    def _():
        z = jnp.zeros((BLK, ATTN_KV_HEADS * LANES), BF16)
        klo_s[0:BLK, :] = z
        khi_s[0:BLK, :] = z
        vlo_s[0:BLK, :] = z
        vhi_s[0:BLK, :] = z
        st_s[...] = jnp.zeros_like(st_s)

    kn = qk_norm(jnp.dot(hb, win_ref[:, OFF_AK:OFF_AK + KV_WIDTH], preferred_element_type=F32),
                 kw_ref[...])
    vv = jnp.dot(hb, win_ref[:, OFF_AV:OFF_AV + KV_WIDTH], preferred_element_type=F32)
    for half in range(2):
        store_split(klo_s, khi_s, rope(kn[:, half * LANES:(half + 1) * LANES]), half)
        store_split(vlo_s, vhi_s, vv[:, half * LANES:(half + 1) * LANES], half)

    ag = jnp.dot(hb, win_ref[:, OFF_AG:OFF_AG + ATTN_WIDTH], preferred_element_type=F32)
    ga_s[...] = (ag * _sigmoid(ag)).astype(BF16)
    rq_s[...] = jnp.dot(hb, win_ref[:, OFF_RQ:OFF_RQ + REC_WIDTH], preferred_element_type=F32)
    rf_s[...] = jnp.dot(hb, win_ref[:, OFF_RF:OFF_RF + REC_WIDTH], preferred_element_type=F32)
    vr_s[...] = jnp.dot(hb, win_ref[:, OFF_RI:OFF_RI + REC_WIDTH],
                        preferred_element_type=F32).astype(BF16)
    rg = jnp.dot(hb, win_ref[:, OFF_RG:OFF_RG + REC_WIDTH], preferred_element_type=F32)
    gr_s[...] = (rg * _sigmoid(rg)).astype(BF16)

    row_i = lax.broadcasted_iota(jnp.int32, (BLK, BLK), 0)
    col_i = lax.broadcasted_iota(jnp.int32, (BLK, BLK), 1)
    causal = row_i >= col_i
    lane_b = lax.broadcasted_iota(jnp.int32, (BLK, LANES), 1)
    low_head_b = lane_b < HEAD_DIM

    def block_body(blk, carry):
        r0 = pl.multiple_of(blk * BLK, BLK)
        rows = pl.ds(r0, BLK)
        win_rows = pl.ds(r0, 2 * BLK)
        seq_start = jnp.logical_and(t == 0, blk == 0)
        bias = bias_ref[jnp.where(seq_start, 1, 0)]

        for j in range(ATTN_KV_HEADS):
            kv_cols = slice(j * LANES, (j + 1) * LANES)
            p2 = jnp.concatenate(
                [q_s[rows, (2 * j) * LANES:(2 * j + 1) * LANES],
                 q_s[rows, (2 * j + 1) * LANES:(2 * j + 2) * LANES]], axis=0)
            k2 = jnp.concatenate([klo_s[win_rows, kv_cols], khi_s[win_rows, kv_cols]], axis=0)
            v2 = jnp.concatenate([vlo_s[win_rows, kv_cols], vhi_s[win_rows, kv_cols]], axis=0)
            s2 = lax.dot_general(p2, k2, _NT, preferred_element_type=F32) + bias
            p_rows = []
            inv_rows = []
            for r in range(2):
                p_cols = []
                inv_cols = []
                for e in range(2):
                    sq = s2[r * BLK:(r + 1) * BLK, e * 2 * BLK:(e + 1) * 2 * BLK]
                    sink = sinks_ref[4 * j + 2 * r + e]
                    m = jnp.maximum(jnp.max(sq, axis=1, keepdims=True), sink)
                    p = jnp.exp(sq - m)
                    denom = jnp.sum(p, axis=1, keepdims=True) + jnp.exp(sink - m)
                    p_cols.append(p.astype(BF16))
                    inv_cols.append(1.0 / denom)
                p_rows.append(jnp.concatenate(p_cols, axis=1))
                inv_rows.append(jnp.where(low_head_b, inv_cols[0], inv_cols[1]))
            pm = jnp.concatenate(p_rows, axis=0)
            o2 = jnp.dot(pm, v2, preferred_element_type=F32)
            for r in range(2):
                cols = slice((2 * j + r) * LANES, (2 * j + r + 1) * LANES)
                out = o2[r * BLK:(r + 1) * BLK, :] * inv_rows[r] * ga_s[rows, cols].astype(F32)
                mix_s[rows, cols] = out.astype(BF16)

        for hh in range(REC_HEADS):
            cols = slice(hh * REC_DIM, (hh + 1) * REC_DIM)
            lbh = lb_ref[:, cols]
            f = lbh + (1.0 - lbh) * _sigmoid(rf_s[rows, cols])
            lf = jnp.log(f)
            kk = 1.0 - f
            lf_hi = lf.astype(BF16)
            lf_lo = (lf - lf_hi.astype(F32)).astype(BF16)
            brel = jnp.dot(mrel_ref[...], jnp.concatenate([lf_hi, lf_lo], axis=0),
                           preferred_element_type=F32)
            epos = jnp.exp(brel)
            eneg = jnp.exp(-brel)
            d_mid = f[0:1, :] * eneg[0:1, :]
            d_tail = epos[BLK - 1:BLK, :]
            q_rel = rq_s[rows, cols] * epos
            q_abs = (q_rel * d_mid).astype(BF16)
            k_inv = kk * eneg
            k_out = (k_inv * d_tail).astype(BF16)
            a = lax.dot_general(q_rel.astype(BF16), k_inv.astype(BF16), _NT,
                                preferred_element_type=F32)
            a = jnp.where(causal, a, 0.0).astype(BF16)
            vt = vr_s[rows, cols].T
            st = st_s[hh]
            o = lax.dot_general(jnp.concatenate([a, q_abs], axis=1),
                                jnp.concatenate([vt, st.astype(BF16)], axis=1), _NT,
                                preferred_element_type=F32)
            st_s[hh] = st * (d_tail * d_mid) + jnp.dot(vt, k_out, preferred_element_type=F32)
            rr = lax.rsqrt(jnp.mean(o * o, axis=1, keepdims=True) + EPS)
            rec = (o * rr) * recw_ref[...] * gr_s[rows, cols].astype(F32)
            mix_s[rows, ATTN_WIDTH + hh * REC_DIM:ATTN_WIDTH + (hh + 1) * REC_DIM] = rec.astype(BF16)
        return carry

    lax.fori_loop(0, nblk, block_body, 0)

    klo_s[0:BLK, :] = klo_s[T:T + BLK, :]
    khi_s[0:BLK, :] = khi_s[T:T + BLK, :]
    vlo_s[0:BLK, :] = vlo_s[T:T + BLK, :]
    vhi_s[0:BLK, :] = vhi_s[T:T + BLK, :]

    gate = mod[:, 2 * D_MODEL:3 * D_MODEL]
    y = jnp.dot(mix_s[...], wout_ref[...], preferred_element_type=F32)
    o_ref[0] = x + gate * y


def _constants():
    head_of = np.arange(256) // HEAD_DIM
    bd = (head_of[:, None] == head_of[None, :]).astype(np.float32)
    mid = BLK // 2 - 1
    tt = np.arange(BLK)[:, None]
    rr = np.arange(BLK)[None, :]
    mrel = ((rr <= tt).astype(np.float32) - (rr <= mid).astype(np.float32))
    mrel2 = np.concatenate([mrel, mrel], axis=1)
    qi = np.arange(BLK)[:, None]
    kj = np.arange(2 * BLK)[None, :]
    dist = BLK + qi - kj
    band = (dist >= 0) & (dist < WINDOW)
    band0 = band & (kj >= BLK)
    def tile_bias(mask):
        b = np.where(mask, 0.0, -np.inf).astype(np.float32)
        b = np.concatenate([b, b], axis=1)
        return np.concatenate([b, b], axis=0)
    bias = np.stack([tile_bias(band), tile_bias(band0)])
    lane = np.arange(LANES)
    sgn = np.where((lane % HEAD_DIM) < HEAD_DIM // 2, -1.0, 1.0).astype(np.float32)[None, :]
    return bd, mrel2, bias, sgn


def kernel(x, c, positions, norm_w, w_ada, b_ada, w_in, q_norm_w, k_norm_w, sinks, rec_norm_w,
           lower_bounds, w_out):
    B, S, D = x.shape
    assert D == D_MODEL and w_in.shape == (1, D_MODEL, IN_WIDTH) and S % TOKEN_TILE == 0
    assert lower_bounds.shape[0] == 2 and w_out.shape == (1, ATTN_WIDTH + REC_WIDTH, D_MODEL)
    T = TOKEN_TILE

    n_ada = 3
    mod, lb = pl.pallas_call(
        _ada_kernel,
        grid=(n_ada,),
        in_specs=[
            pl.BlockSpec((B, D), lambda i: (0, 0)),
            pl.BlockSpec((D, D), lambda i: (0, i)),
            pl.BlockSpec((1, D), lambda i: (0, i)),
            pl.BlockSpec((2, REC_WIDTH), lambda i: (0, 0)),
        ],
        out_specs=[
            pl.BlockSpec((B, D), lambda i: (0, i)),
            pl.BlockSpec((1, REC_WIDTH), lambda i: (0, 0)),
        ],
        out_shape=[jax.ShapeDtypeStruct((B, 3 * D), F32),
                   jax.ShapeDtypeStruct((1, REC_WIDTH), F32)],
        compiler_params=pltpu.CompilerParams(dimension_semantics=("arbitrary",)),
        name="ada_modulation",
    )(c, w_ada[0], b_ada, lower_bounds)

    bd, mrel2, bias, sgn = _constants()
    inv_freq = ROPE_THETA ** (-jnp.arange(0, HEAD_DIM, 2, dtype=F32) / HEAD_DIM)
    invf = jnp.tile(inv_freq, LANES // (HEAD_DIM // 2))[None, :]
    qw = jnp.tile(q_norm_w[0], 256 // HEAD_DIM)[None, :] * (HEAD_DIM ** -0.5)
    kw = jnp.tile(k_norm_w[0], 256 // HEAD_DIM)[None, :]

    const = lambda shape: pl.BlockSpec(shape, lambda b, t: (0,) * len(shape))
    resident = lambda shape: pl.BlockSpec(shape, lambda b, t: (0,) * len(shape),
                                          pipeline_mode=pl.Buffered(1))
    kv_lanes = ATTN_KV_HEADS * LANES
    out = pl.pallas_call(
        _layer_kernel,
        grid=(B, S // T),
        in_specs=[
            pl.BlockSpec(memory_space=pltpu.SMEM),
            pl.BlockSpec((1, T, D), lambda b, t: (b, t, 0)),
            pl.BlockSpec((1, T, 1), lambda b, t: (b, t, 0)),
            pl.BlockSpec((1, 1, 3 * D), lambda b, t: (b, 0, 0)),
            const((1, D)),
            resident((D, IN_WIDTH)),
            resident((ATTN_WIDTH + REC_WIDTH, D)),
            const((1, 256)), const((1, 256)), const((1, REC_DIM)),
            const((1, REC_WIDTH)),
            const((1, LANES)), const((1, LANES)),
            const((256, 256)), const((BLK, 2 * BLK)),
            const((2, 2 * BLK, 4 * BLK)),
        ],
        out_specs=pl.BlockSpec((1, T, D), lambda b, t: (b, t, 0)),
        out_shape=jax.ShapeDtypeStruct((B, S, D), x.dtype),
        scratch_shapes=[
            pltpu.VMEM((T, D), BF16),
            pltpu.VMEM((T, LANES), F32),
            pltpu.VMEM((T, LANES), F32),
            pltpu.VMEM((T, ATTN_WIDTH), BF16),
            pltpu.VMEM((T + BLK, kv_lanes), BF16),
            pltpu.VMEM((T + BLK, kv_lanes), BF16),
            pltpu.VMEM((T + BLK, kv_lanes), BF16),
            pltpu.VMEM((T + BLK, kv_lanes), BF16),
            pltpu.VMEM((T, ATTN_WIDTH), BF16),
            pltpu.VMEM((T, REC_WIDTH), F32),
            pltpu.VMEM((T, REC_WIDTH), F32),
            pltpu.VMEM((T, REC_WIDTH), BF16),
            pltpu.VMEM((T, REC_WIDTH), BF16),
            pltpu.VMEM((REC_HEADS, REC_DIM, REC_DIM), F32),
            pltpu.VMEM((T, ATTN_WIDTH + REC_WIDTH), BF16),
        ],
        compiler_params=pltpu.CompilerParams(
            dimension_semantics=("arbitrary", "arbitrary"),
            vmem_limit_bytes=VMEM_LIMIT_BYTES),
        name="hybrid_layer",
    )(sinks[0], x, positions[:, :, None], mod[:, None, :], norm_w, w_in[0].astype(BF16),
      w_out[0].astype(BF16), qw, kw, rec_norm_w, lb, invf, jnp.asarray(sgn),
      jnp.asarray(bd, BF16), jnp.asarray(mrel2, BF16), jnp.asarray(bias))
    return out
```

```python
import functools

import jax
import jax.numpy as jnp
import numpy as np
from jax import lax
from jax.experimental import pallas as pl
from jax.experimental.pallas import tpu as pltpu

F32 = jnp.float32
BF16 = jnp.bfloat16

D_MODEL = 1024
ATTN_HEADS = 16
ATTN_KV_HEADS = 4
HEAD_DIM = 64
KV_WIDTH = ATTN_KV_HEADS * HEAD_DIM
ATTN_WIDTH = ATTN_HEADS * HEAD_DIM
WINDOW = 128
ROPE_THETA = 10000.0
REC_HEADS = 8
REC_DIM = 128
REC_WIDTH = REC_HEADS * REC_DIM
EPS = 1e-6
BLK = 128
LANES = 128

OFF_AQ = 0
OFF_AK = OFF_AQ + ATTN_WIDTH
OFF_AV = OFF_AK + KV_WIDTH
OFF_AG = OFF_AV + KV_WIDTH
OFF_RQ = OFF_AG + ATTN_WIDTH
OFF_RF = OFF_RQ + REC_WIDTH
OFF_RI = OFF_RF + REC_WIDTH
OFF_RG = OFF_RI + REC_WIDTH
IN_WIDTH = OFF_RG + REC_WIDTH

TOKEN_TILE = 512
VMEM_LIMIT_BYTES = 56 * 1024 * 1024

_NT = (((1,), (1,)), ((), ()))


def _sigmoid(x):
    return 1.0 / (1.0 + jnp.exp(-x))


def _ada_kernel(c_ref, w_ref, b_ref, lbin_ref, mod_ref, lb_ref):
    c = c_ref[...]
    c_act = c * _sigmoid(c)
    mod_ref[...] = jnp.dot(c_act.astype(BF16), w_ref[...].astype(BF16),
                           preferred_element_type=F32) + b_ref[...]
    lbv = lbin_ref[...]
    e = jnp.exp(lbv - jnp.max(lbv, axis=0, keepdims=True))
    lb_ref[...] = e[0:1, :] / jnp.sum(e, axis=0, keepdims=True)


def _layer_kernel(sinks_ref, x_ref, pos_ref, mod_ref, normw_ref, win_ref, wout_ref,
                  qw_ref, kw_ref, recw_ref, lb_ref, invf_ref, sgn_ref, bd_ref, mrel_ref,
                  bias_ref, o_ref,
                  hb_s, cos_s, sin_s, q_s, kx_s, vx_s, ga_s,
                  rq_s, rf_s, vr_s, gr_s, st_s, mix_s):
    T = x_ref.shape[1]
    nblk = T // BLK
    t = pl.program_id(1)

    x = x_ref[0]
    mod = mod_ref[0]
    shift = mod[:, 0:D_MODEL]
    scale = mod[:, D_MODEL:2 * D_MODEL]
    ms = jnp.mean(x * x, axis=-1, keepdims=True)
    h = (x * lax.rsqrt(ms + EPS)) * (normw_ref[...] * (1.0 + scale)) + shift
    hb_s[...] = h.astype(BF16)

    ang = pos_ref[0].astype(F32) * invf_ref[...]
    cos_s[...] = jnp.cos(ang)
    sin_s[...] = jnp.sin(ang) * sgn_ref[...]

    lane = lax.broadcasted_iota(jnp.int32, (T, LANES), 1)
    first_half = (lane & (HEAD_DIM // 2)) == 0
    low_head = lane < HEAD_DIM

    def qk_norm(raw, w_row):
        ssq = jnp.dot((raw * raw).astype(BF16), bd_ref[...], preferred_element_type=F32)
        return raw * lax.rsqrt(ssq * (1.0 / HEAD_DIM) + EPS) * w_row

    def rope(v):
        partner = jnp.where(first_half, pltpu.roll(v, LANES - HEAD_DIM // 2, 1),
                            pltpu.roll(v, HEAD_DIM // 2, 1))
        return v * cos_s[...] + partner * sin_s[...]

    hb = hb_s[...]

    for cb in range(ATTN_WIDTH // 256):
        c0 = OFF_AQ + cb * 256
        qn = qk_norm(jnp.dot(hb, win_ref[:, c0:c0 + 256], preferred_element_type=F32), qw_ref[...])
        for half in range(2):
            q_s[:, cb * 256 + half * LANES:cb * 256 + (half + 1) * LANES] = rope(
                qn[:, half * LANES:(half + 1) * LANES]).astype(BF16)

    def store_split(dst_s, pair, kv_pair):
        sw = pltpu.roll(pair, HEAD_DIM, 1)
        zero = jnp.zeros_like(pair)
        parts = ((2 * kv_pair, jnp.where(low_head, pair, zero), jnp.where(low_head, zero, sw)),
                 (2 * kv_pair + 1, jnp.where(low_head, sw, zero), jnp.where(low_head, zero, pair)))
        for head, lo, hi in parts:
            lo = lo.astype(BF16)
            hi = hi.astype(BF16)
            for blk in range(nblk):
                base = (blk + 1) * 2 * BLK
                dst_s[head, base:base + BLK, :] = lo[blk * BLK:(blk + 1) * BLK, :]
                dst_s[head, base + BLK:base + 2 * BLK, :] = hi[blk * BLK:(blk + 1) * BLK, :]

    @pl.when(t == 0)
    def _():
        z = jnp.zeros((ATTN_KV_HEADS, 2 * BLK, LANES), BF16)
        kx_s[:, 0:2 * BLK, :] = z
        vx_s[:, 0:2 * BLK, :] = z
        st_s[...] = jnp.zeros_like(st_s)

    kn = qk_norm(jnp.dot(hb, win_ref[:, OFF_AK:OFF_AK + KV_WIDTH], preferred_element_type=F32),
                 kw_ref[...])
    vv = jnp.dot(hb, win_ref[:, OFF_AV:OFF_AV + KV_WIDTH], preferred_element_type=F32)
    for half in range(2):
        store_split(kx_s, rope(kn[:, half * LANES:(half + 1) * LANES]), half)
        store_split(vx_s, vv[:, half * LANES:(half + 1) * LANES], half)

    ag = jnp.dot(hb, win_ref[:, OFF_AG:OFF_AG + ATTN_WIDTH], preferred_element_type=F32)
    ga_s[...] = (ag * _sigmoid(ag)).astype(BF16)
    rq_s[...] = jnp.dot(hb, win_ref[:, OFF_RQ:OFF_RQ + REC_WIDTH], preferred_element_type=F32)
    rf_s[...] = jnp.dot(hb, win_ref[:, OFF_RF:OFF_RF + REC_WIDTH], preferred_element_type=F32)
    vr_s[...] = jnp.dot(hb, win_ref[:, OFF_RI:OFF_RI + REC_WIDTH],
                        preferred_element_type=F32).astype(BF16)
    rg = jnp.dot(hb, win_ref[:, OFF_RG:OFF_RG + REC_WIDTH], preferred_element_type=F32)
    gr_s[...] = (rg * _sigmoid(rg)).astype(BF16)

    row_i = lax.broadcasted_iota(jnp.int32, (BLK, BLK), 0)
    col_i = lax.broadcasted_iota(jnp.int32, (BLK, BLK), 1)
    causal = row_i >= col_i
    lane_b = lax.broadcasted_iota(jnp.int32, (BLK, LANES), 1)
    low_head_b = lane_b < HEAD_DIM
    lb_row = lb_ref[...]

    def mix_block(blk):
        rows = slice(blk * BLK, (blk + 1) * BLK)
        win_rows = slice(blk * 2 * BLK, (blk + 2) * 2 * BLK)
        if blk == 0:
            bias = bias_ref[jnp.where(t == 0, 1, 0)]
        else:
            bias = bias_ref[0]

        s2 = []
        for j in range(ATTN_KV_HEADS):
            p2 = jnp.concatenate(
                [q_s[rows, (2 * j) * LANES:(2 * j + 1) * LANES],
                 q_s[rows, (2 * j + 1) * LANES:(2 * j + 2) * LANES]], axis=0)
            s2.append(lax.dot_general(p2, kx_s[j, win_rows, :], _NT,
                                      preferred_element_type=F32))

        f = lb_row + (1.0 - lb_row) * _sigmoid(rf_s[rows, :])
        lf = jnp.log(f)
        kk = 1.0 - f
        lf_hi = lf.astype(BF16)
        lf_lo = (lf - lf_hi.astype(F32)).astype(BF16)
        brel = jnp.dot(mrel_ref[...], jnp.concatenate([lf_hi, lf_lo], axis=0),
                       preferred_element_type=F32)

        pm = []
        inv = []
        for j in range(ATTN_KV_HEADS):
            p_rows = []
            for r in range(2):
                sr = s2[j][r * BLK:(r + 1) * BLK, :] + bias
                p_prev = []
                p_cur = []
                inv_cols = []
                for e in range(2):
                    sp = sr[:, e * BLK:(e + 1) * BLK]
                    sc = sr[:, (2 + e) * BLK:(3 + e) * BLK]
                    sink = sinks_ref[4 * j + 2 * r + e]
                    m = jnp.maximum(jnp.max(jnp.maximum(sp, sc), axis=1, keepdims=True), sink)
                    pp = jnp.exp(sp - m)
                    pc = jnp.exp(sc - m)
                    denom = jnp.sum(pp + pc, axis=1, keepdims=True) + jnp.exp(sink - m)
                    p_prev.append(pp.astype(BF16))
                    p_cur.append(pc.astype(BF16))
                    inv_cols.append(1.0 / denom)
                p_rows.append(jnp.concatenate(p_prev + p_cur, axis=1))
                inv.append(jnp.where(low_head_b, inv_cols[0], inv_cols[1]))
            pm.append(jnp.concatenate(p_rows, axis=0))

        epos = jnp.exp(brel)
        eneg = jnp.exp(-brel)
        d_mid = f[0:1, :] * eneg[0:1, :]
        d_tail = epos[BLK - 1:BLK, :]
        q_rel = rq_s[rows, :] * epos
        q_abs = (q_rel * d_mid).astype(BF16)
        q_rel = q_rel.astype(BF16)
        k_inv = kk * eneg
        k_out = (k_inv * d_tail).astype(BF16)
        k_inv = k_inv.astype(BF16)
        d_all = d_tail * d_mid

        o2 = [jnp.dot(pm[j], vx_s[j, win_rows, :], preferred_element_type=F32)
              for j in range(ATTN_KV_HEADS)]

        a_mats = []
        vts = []
        us = []
        for hh in range(REC_HEADS):
            cols = slice(hh * REC_DIM, (hh + 1) * REC_DIM)
            a_mats.append(lax.dot_general(q_rel[:, cols], k_inv[:, cols], _NT,
                                          preferred_element_type=F32))
            vt = vr_s[rows, cols].T
            vts.append(vt)
            us.append(jnp.dot(vt, k_out[:, cols], preferred_element_type=F32))

        for j in range(ATTN_KV_HEADS):
            for r in range(2):
                cols = slice((2 * j + r) * LANES, (2 * j + r + 1) * LANES)
                out = o2[j][r * BLK:(r + 1) * BLK, :] * inv[2 * j + r] * ga_s[rows, cols].astype(F32)
                mix_s[rows, cols] = out.astype(BF16)

        outs = []
        for hh in range(REC_HEADS):
            cols = slice(hh * REC_DIM, (hh + 1) * REC_DIM)
            a = jnp.where(causal, a_mats[hh], 0.0).astype(BF16)
            st = st_s[hh]
            outs.append(lax.dot_general(jnp.concatenate([a, q_abs[:, cols]], axis=1),
                                        jnp.concatenate([vts[hh], st.astype(BF16)], axis=1), _NT,
                                        preferred_element_type=F32))
            st_s[hh] = st * d_all[:, cols] + us[hh]

        for hh in range(REC_HEADS):
            cols = slice(hh * REC_DIM, (hh + 1) * REC_DIM)
            o = outs[hh]
            rr = lax.rsqrt(jnp.mean(o * o, axis=1, keepdims=True) + EPS)
            rec = (o * rr) * recw_ref[...] * gr_s[rows, cols].astype(F32)
            mix_s[rows, ATTN_WIDTH + hh * REC_DIM:ATTN_WIDTH + (hh + 1) * REC_DIM] = rec.astype(BF16)

    for blk in range(nblk):
        mix_block(blk)

    kx_s[:, 0:2 * BLK, :] = kx_s[:, nblk * 2 * BLK:(nblk + 1) * 2 * BLK, :]
    vx_s[:, 0:2 * BLK, :] = vx_s[:, nblk * 2 * BLK:(nblk + 1) * 2 * BLK, :]

    gate = mod[:, 2 * D_MODEL:3 * D_MODEL]
    y = jnp.dot(mix_s[...], wout_ref[...], preferred_element_type=F32)
    o_ref[0] = x + gate * y


def _constants():
    head_of = np.arange(256) // HEAD_DIM
    bd = (head_of[:, None] == head_of[None, :]).astype(np.float32)
    mid = BLK // 2 - 1
    tt = np.arange(BLK)[:, None]
    rr = np.arange(BLK)[None, :]
    mrel = ((rr <= tt).astype(np.float32) - (rr <= mid).astype(np.float32))
    mrel2 = np.concatenate([mrel, mrel], axis=1)
    assert WINDOW == BLK
    qi = np.arange(BLK)[:, None]
    kj = np.arange(BLK)[None, :]
    ninf = np.float32(-np.inf)
    prev = np.where(kj > qi, 0.0, ninf).astype(np.float32)
    cur = np.where(kj <= qi, 0.0, ninf).astype(np.float32)
    none = np.full((BLK, BLK), ninf, np.float32)
    bias = np.stack([np.concatenate([prev, prev, cur, cur], axis=1),
                     np.concatenate([none, none, cur, cur], axis=1)])
    lane = np.arange(LANES)
    sgn = np.where((lane % HEAD_DIM) < HEAD_DIM // 2, -1.0, 1.0).astype(np.float32)[None, :]
    return bd, mrel2, bias, sgn


def kernel(x, c, positions, norm_w, w_ada, b_ada, w_in, q_norm_w, k_norm_w, sinks, rec_norm_w,
           lower_bounds, w_out):
    B, S, D = x.shape
    assert D == D_MODEL and w_in.shape == (1, D_MODEL, IN_WIDTH) and S % TOKEN_TILE == 0
    assert lower_bounds.shape[0] == 2 and w_out.shape == (1, ATTN_WIDTH + REC_WIDTH, D_MODEL)
    T = TOKEN_TILE

    n_ada = 3
    mod, lb = pl.pallas_call(
        _ada_kernel,
        grid=(n_ada,),
        in_specs=[
            pl.BlockSpec((B, D), lambda i: (0, 0)),
            pl.BlockSpec((D, D), lambda i: (0, i)),
            pl.BlockSpec((1, D), lambda i: (0, i)),
            pl.BlockSpec((2, REC_WIDTH), lambda i: (0, 0)),
        ],
        out_specs=[
            pl.BlockSpec((B, D), lambda i: (0, i)),
            pl.BlockSpec((1, REC_WIDTH), lambda i: (0, 0)),
        ],
        out_shape=[jax.ShapeDtypeStruct((B, 3 * D), F32),
                   jax.ShapeDtypeStruct((1, REC_WIDTH), F32)],
        compiler_params=pltpu.CompilerParams(dimension_semantics=("arbitrary",)),
        name="ada_modulation",
    )(c, w_ada[0], b_ada, lower_bounds)

    bd, mrel2, bias, sgn = _constants()
    inv_freq = ROPE_THETA ** (-jnp.arange(0, HEAD_DIM, 2, dtype=F32) / HEAD_DIM)
    invf = jnp.tile(inv_freq, LANES // (HEAD_DIM // 2))[None, :]
    qw = jnp.tile(q_norm_w[0], 256 // HEAD_DIM)[None, :] * (HEAD_DIM ** -0.5)
    kw = jnp.tile(k_norm_w[0], 256 // HEAD_DIM)[None, :]

    const = lambda shape: pl.BlockSpec(shape, lambda b, t: (0,) * len(shape))
    resident = lambda shape: pl.BlockSpec(shape, lambda b, t: (0,) * len(shape),
                                          pipeline_mode=pl.Buffered(1))
    out = pl.pallas_call(
        _layer_kernel,
        grid=(B, S // T),
        in_specs=[
            pl.BlockSpec(memory_space=pltpu.SMEM),
            pl.BlockSpec((1, T, D), lambda b, t: (b, t, 0)),
            pl.BlockSpec((1, T, 1), lambda b, t: (b, t, 0)),
            pl.BlockSpec((1, 1, 3 * D), lambda b, t: (b, 0, 0)),
            const((1, D)),
            resident((D, IN_WIDTH)),
            resident((ATTN_WIDTH + REC_WIDTH, D)),
            const((1, 256)), const((1, 256)), const((1, REC_DIM)),
            const((1, REC_WIDTH)),
            const((1, LANES)), const((1, LANES)),
            const((256, 256)), const((BLK, 2 * BLK)),
            const((2, BLK, 4 * BLK)),
        ],
        out_specs=pl.BlockSpec((1, T, D), lambda b, t: (b, t, 0)),
        out_shape=jax.ShapeDtypeStruct((B, S, D), x.dtype),
        scratch_shapes=[
            pltpu.VMEM((T, D), BF16),
            pltpu.VMEM((T, LANES), F32),
            pltpu.VMEM((T, LANES), F32),
            pltpu.VMEM((T, ATTN_WIDTH), BF16),
            pltpu.VMEM((ATTN_KV_HEADS, (T // BLK + 1) * 2 * BLK, LANES), BF16),
            pltpu.VMEM((ATTN_KV_HEADS, (T // BLK + 1) * 2 * BLK, LANES), BF16),
            pltpu.VMEM((T, ATTN_WIDTH), BF16),
            pltpu.VMEM((T, REC_WIDTH), F32),
            pltpu.VMEM((T, REC_WIDTH), F32),
            pltpu.VMEM((T, REC_WIDTH), BF16),
            pltpu.VMEM((T, REC_WIDTH), BF16),
            pltpu.VMEM((REC_HEADS, REC_DIM, REC_DIM), F32),
            pltpu.VMEM((T, ATTN_WIDTH + REC_WIDTH), BF16),
        ],
        compiler_params=pltpu.CompilerParams(
            dimension_semantics=("arbitrary", "arbitrary"),
            vmem_limit_bytes=VMEM_LIMIT_BYTES),
        name="hybrid_layer",
    )(sinks[0], x, positions[:, :, None], mod[:, None, :], norm_w, w_in[0].astype(BF16),
      w_out[0].astype(BF16), qw, kw, rec_norm_w, lb, invf, jnp.asarray(sgn),
      jnp.asarray(bd, BF16), jnp.asarray(mrel2, BF16), jnp.asarray(bias))
    return out
```

```python
import jax
import jax.numpy as jnp
import numpy as np
from jax import lax
from jax.experimental import pallas as pl
from jax.experimental.pallas import tpu as pltpu

F32 = jnp.float32
BF16 = jnp.bfloat16

D_MODEL = 1024
ATTN_HEADS = 16
ATTN_KV_HEADS = 4
HEAD_DIM = 64
KV_WIDTH = ATTN_KV_HEADS * HEAD_DIM
ATTN_WIDTH = ATTN_HEADS * HEAD_DIM
WINDOW = 128
ROPE_THETA = 10000.0
REC_HEADS = 8
REC_DIM = 128
REC_WIDTH = REC_HEADS * REC_DIM
EPS = 1e-6
BLK = 128
LANES = 128
UNIT = 2 * BLK

OFF_AQ = 0
OFF_AK = OFF_AQ + ATTN_WIDTH
OFF_AV = OFF_AK + KV_WIDTH
OFF_AG = OFF_AV + KV_WIDTH
OFF_RQ = OFF_AG + ATTN_WIDTH
OFF_RF = OFF_RQ + REC_WIDTH
OFF_RI = OFF_RF + REC_WIDTH
OFF_RG = OFF_RI + REC_WIDTH
IN_WIDTH = OFF_RG + REC_WIDTH

TOKEN_TILE = 512
VMEM_LIMIT_BYTES = 58 * 1024 * 1024

_NT = (((1,), (1,)), ((), ()))


def _sigmoid(x):
    return 0.5 * jnp.tanh(0.5 * x) + 0.5


def _silu(x):
    hx = 0.5 * x
    return hx * jnp.tanh(hx) + hx


def _ada_kernel(c_ref, w_ref, b_ref, lbin_ref, mod_ref, lb_ref):
    mod_ref[...] = jnp.dot(_silu(c_ref[...]).astype(BF16), w_ref[...].astype(BF16),
                           preferred_element_type=F32) + b_ref[...]
    lbv = lbin_ref[...]
    e = jnp.exp(lbv - jnp.max(lbv, axis=0, keepdims=True))
    lb_ref[...] = e[0:1, :] / jnp.sum(e, axis=0, keepdims=True)


def _layer_kernel(x_ref, pos_ref, mod_ref, normw_ref, win_ref, wout_ref,
                  qw_ref, kw_ref, recw_ref, lb_ref, invf_ref, phase_ref, bd_ref, mrel_ref,
                  sinkb_ref, curb_ref, ones_ref, o_ref,
                  hb_s, cos_s, sin_s, q_s, kx_s, vx_s, ga_s,
                  rq_s, rf_s, vr_s, gr_s, st_s, mix_s):
    T = x_ref.shape[1]
    nblk = T // BLK
    t = pl.program_id(1)

    x = x_ref[0]
    mod = mod_ref[0]
    shift = mod[:, 0:D_MODEL]
    scale = mod[:, D_MODEL:2 * D_MODEL]
    ms = jnp.mean(x * x, axis=-1, keepdims=True)
    h = (x * lax.rsqrt(ms + EPS)) * (normw_ref[...] * (1.0 + scale)) + shift
    hb_s[...] = h.astype(BF16)

    lane = lax.broadcasted_iota(jnp.int32, (T, LANES), 1)
    first_half = (lane & (HEAD_DIM // 2)) == 0
    low_head = lane < HEAD_DIM

    trig = jnp.sin(pos_ref[0].astype(F32) * invf_ref[...] + phase_ref[...])
    cos_s[...] = jnp.where(first_half, trig, pltpu.roll(trig, HEAD_DIM // 2, 1))
    sin_s[...] = jnp.where(first_half, -pltpu.roll(trig, LANES - HEAD_DIM // 2, 1), trig)

    def qk_norm(raw, w_row):
        ssq = jnp.dot((raw * raw).astype(BF16), bd_ref[...], preferred_element_type=F32)
        return raw * lax.rsqrt(ssq * (1.0 / HEAD_DIM) + EPS) * w_row

    def rope(v):
        partner = jnp.where(first_half, pltpu.roll(v, LANES - HEAD_DIM // 2, 1),
                            pltpu.roll(v, HEAD_DIM // 2, 1))
        return v * cos_s[...] + partner * sin_s[...]

    hb = hb_s[...]

    for cb in range(ATTN_WIDTH // 256):
        c0 = OFF_AQ + cb * 256
        qn = qk_norm(jnp.dot(hb, win_ref[:, c0:c0 + 256], preferred_element_type=F32), qw_ref[...])
        for half in range(2):
            q_s[:, cb * 256 + half * LANES:cb * 256 + (half + 1) * LANES] = rope(
                qn[:, half * LANES:(half + 1) * LANES]).astype(BF16)

    row = lax.broadcasted_iota(jnp.int32, (T, LANES), 0)
    keep_prev = (row & (BLK - 1)) != 0

    def store_split(dst_s, pair, kv_pair):
        sw = pltpu.roll(pair, HEAD_DIM, 1)
        zero = jnp.zeros_like(pair)
        parts = ((2 * kv_pair, jnp.where(low_head, pair, zero), jnp.where(low_head, zero, sw)),
                 (2 * kv_pair + 1, jnp.where(low_head, sw, zero), jnp.where(low_head, zero, pair)))
        for head, lo, hi in parts:
            for role_prev in (False, True):
                lo_r = (jnp.where(keep_prev, lo, zero) if role_prev else lo).astype(BF16)
                hi_r = (jnp.where(keep_prev, hi, zero) if role_prev else hi).astype(BF16)
                for blk in range(nblk):
                    base = (2 * blk + (2 if role_prev else 1)) * UNIT
                    dst_s[head, base:base + BLK, 0:LANES] = lo_r[blk * BLK:(blk + 1) * BLK, :]
                    dst_s[head, base + BLK:base + UNIT, 0:LANES] = hi_r[blk * BLK:(blk + 1) * BLK, :]

    @pl.when(t > 0)
    def _():
        kx_s[:, 0:UNIT, :] = kx_s[:, 2 * nblk * UNIT:(2 * nblk + 1) * UNIT, :]
        vx_s[:, 0:UNIT, :] = vx_s[:, 2 * nblk * UNIT:(2 * nblk + 1) * UNIT, :]

    @pl.when(t == 0)
    def _():
        kx_s[:, 0:UNIT, :] = jnp.zeros((ATTN_KV_HEADS, UNIT, LANES), BF16)
        vx_s[:, 0:UNIT, 0:LANES] = jnp.zeros((ATTN_KV_HEADS, UNIT, LANES), BF16)
        st_s[...] = jnp.zeros_like(st_s)
        for head in range(ATTN_KV_HEADS):
            for u in range(2 * nblk + 1):
                vx_s[head, u * UNIT:(u + 1) * UNIT, LANES:2 * LANES] = ones_ref[...]

    kn = qk_norm(jnp.dot(hb, win_ref[:, OFF_AK:OFF_AK + KV_WIDTH], preferred_element_type=F32),
                 kw_ref[...])
    vv = jnp.dot(hb, win_ref[:, OFF_AV:OFF_AV + KV_WIDTH], preferred_element_type=F32)
    for half in range(2):
        store_split(kx_s, rope(kn[:, half * LANES:(half + 1) * LANES]), half)
        store_split(vx_s, vv[:, half * LANES:(half + 1) * LANES], half)

    ga_s[...] = _silu(jnp.dot(hb, win_ref[:, OFF_AG:OFF_AG + ATTN_WIDTH],
                              preferred_element_type=F32)).astype(BF16)
    rq_s[...] = jnp.dot(hb, win_ref[:, OFF_RQ:OFF_RQ + REC_WIDTH], preferred_element_type=F32)
    rf_s[...] = jnp.dot(hb, win_ref[:, OFF_RF:OFF_RF + REC_WIDTH], preferred_element_type=F32)
    vr_s[...] = jnp.dot(hb, win_ref[:, OFF_RI:OFF_RI + REC_WIDTH],
                        preferred_element_type=F32).astype(BF16)
    gr_s[...] = _silu(jnp.dot(hb, win_ref[:, OFF_RG:OFF_RG + REC_WIDTH],
                              preferred_element_type=F32)).astype(BF16)

    row_i = lax.broadcasted_iota(jnp.int32, (BLK, BLK), 0)
    col_i = lax.broadcasted_iota(jnp.int32, (BLK, BLK), 1)
    causal = row_i >= col_i
    lb_row = lb_ref[...]

    def mix_block(blk):
        rows = slice(blk * BLK, (blk + 1) * BLK)
        win_rows = slice(2 * blk * UNIT, (2 * blk + 2) * UNIT)

        s2 = []
        for j in range(ATTN_KV_HEADS):
            p2 = jnp.concatenate(
                [q_s[rows, (2 * j) * LANES:(2 * j + 1) * LANES],
                 q_s[rows, (2 * j + 1) * LANES:(2 * j + 2) * LANES]], axis=0)
            s2.append(lax.dot_general(p2, kx_s[j, win_rows, :], _NT,
                                      preferred_element_type=F32))

        f = lb_row + (1.0 - lb_row) * _sigmoid(rf_s[rows, :])
        lf = jnp.log(f)
        kk = 1.0 - f
        lf_hi = lf.astype(BF16)
        lf_lo = (lf - lf_hi.astype(F32)).astype(BF16)
        brel = jnp.dot(mrel_ref[...], jnp.concatenate([lf_hi, lf_lo], axis=0),
                       preferred_element_type=F32)

        pm = []
        for j in range(ATTN_KV_HEADS):
            p_rows = []
            for r in range(2):
                p_prev = []
                p_cur = []
                for e in range(2):
                    head = 4 * j + 2 * r + e
                    prev_bias = sinkb_ref[0, head]
                    if blk == 0:
                        prev_bias = jnp.where(t == 0, sinkb_ref[1, head], prev_bias)
                    sp = s2[j][r * BLK:(r + 1) * BLK, e * BLK:(e + 1) * BLK] + prev_bias
                    sc = s2[j][r * BLK:(r + 1) * BLK, (2 + e) * BLK:(3 + e) * BLK] + curb_ref[...]
                    m = jnp.max(jnp.maximum(sp, sc), axis=1, keepdims=True)
                    p_prev.append(jnp.exp(sp - m).astype(BF16))
                    p_cur.append(jnp.exp(sc - m).astype(BF16))
                p_rows.append(jnp.concatenate(p_prev + p_cur, axis=1))
            pm.append(jnp.concatenate(p_rows, axis=0))

        epos = jnp.exp(brel)
        eneg = jnp.exp(-brel)
        d_mid = f[0:1, :] * eneg[0:1, :]
        d_tail = epos[BLK - 1:BLK, :]
        q_rel = rq_s[rows, :] * epos
        q_abs = (q_rel * d_mid).astype(BF16)
        q_rel = q_rel.astype(BF16)
        k_inv = kk * eneg
        k_out = (k_inv * d_tail).astype(BF16)
        k_inv = k_inv.astype(BF16)
        d_all = d_tail * d_mid

        o2 = [jnp.dot(pm[j], vx_s[j, win_rows, :], preferred_element_type=F32)
              for j in range(ATTN_KV_HEADS)]

        a_mats = []
        vts = []
        us = []
        for hh in range(REC_HEADS):
            cols = slice(hh * REC_DIM, (hh + 1) * REC_DIM)
            a_mats.append(lax.dot_general(q_rel[:, cols], k_inv[:, cols], _NT,
                                          preferred_element_type=F32))
            vt = vr_s[rows, cols].T
            vts.append(vt)
            us.append(jnp.dot(vt, k_out[:, cols], preferred_element_type=F32))

        for j in range(ATTN_KV_HEADS):
            for r in range(2):
                cols = slice((2 * j + r) * LANES, (2 * j + r + 1) * LANES)
                num = o2[j][r * BLK:(r + 1) * BLK, 0:LANES]
                den = o2[j][r * BLK:(r + 1) * BLK, LANES:2 * LANES]
                mix_s[rows, cols] = (num * (1.0 / den) * ga_s[rows, cols].astype(F32)).astype(BF16)

        outs = []
        for hh in range(REC_HEADS):
            cols = slice(hh * REC_DIM, (hh + 1) * REC_DIM)
            a = jnp.where(causal, a_mats[hh], 0.0).astype(BF16)
            st = st_s[hh]
            outs.append(lax.dot_general(jnp.concatenate([a, q_abs[:, cols]], axis=1),
                                        jnp.concatenate([vts[hh], st.astype(BF16)], axis=1), _NT,
                                        preferred_element_type=F32))
            st_s[hh] = st * d_all[:, cols] + us[hh]

        for hh in range(REC_HEADS):
            cols = slice(hh * REC_DIM, (hh + 1) * REC_DIM)
            o = outs[hh]
            rr = lax.rsqrt(jnp.mean(o * o, axis=1, keepdims=True) + EPS)
            rec = (o * rr) * recw_ref[...] * gr_s[rows, cols].astype(F32)
            mix_s[rows, ATTN_WIDTH + hh * REC_DIM:ATTN_WIDTH + (hh + 1) * REC_DIM] = rec.astype(BF16)

    for blk in range(nblk):
        mix_block(blk)

    gate = mod[:, 2 * D_MODEL:3 * D_MODEL]
    y = jnp.dot(mix_s[...], wout_ref[...], preferred_element_type=F32)
    o_ref[0] = x + gate * y


def _constants():
    head_of = np.arange(256) // HEAD_DIM
    bd = (head_of[:, None] == head_of[None, :]).astype(np.float32)
    mid = BLK // 2 - 1
    tt = np.arange(BLK)[:, None]
    rr = np.arange(BLK)[None, :]
    mrel = ((rr <= tt).astype(np.float32) - (rr <= mid).astype(np.float32))
    mrel2 = np.concatenate([mrel, mrel], axis=1)
    assert WINDOW == BLK
    qi = np.arange(BLK)[:, None]
    kj = np.arange(BLK)[None, :]
    ninf = np.float32(-np.inf)
    prev = np.where(kj > qi, 0.0, ninf).astype(np.float32)
    cur = np.where(kj <= qi, 0.0, ninf).astype(np.float32)
    none = np.full((BLK, BLK), ninf, np.float32)
    lane = np.arange(LANES)
    phase = np.where((lane % HEAD_DIM) < HEAD_DIM // 2, np.pi / 2, 0.0).astype(np.float32)[None, :]
    ones = np.zeros((UNIT, LANES), np.float32)
    ones[0:BLK, 0:HEAD_DIM] = 1.0
    ones[BLK:UNIT, HEAD_DIM:LANES] = 1.0
    return bd, mrel2, np.stack([prev, none]), cur, phase, ones


def kernel(x, c, positions, norm_w, w_ada, b_ada, w_in, q_norm_w, k_norm_w, sinks, rec_norm_w,
           lower_bounds, w_out):
    B, S, D = x.shape
    assert D == D_MODEL and w_in.shape == (1, D_MODEL, IN_WIDTH) and S % TOKEN_TILE == 0
    assert lower_bounds.shape[0] == 2 and w_out.shape == (1, ATTN_WIDTH + REC_WIDTH, D_MODEL)
    T = TOKEN_TILE

    n_ada = 3
    mod, lb = pl.pallas_call(
        _ada_kernel,
        grid=(n_ada,),
        in_specs=[
            pl.BlockSpec((B, D), lambda i: (0, 0)),
            pl.BlockSpec((D, D), lambda i: (0, i)),
            pl.BlockSpec((1, D), lambda i: (0, i)),
            pl.BlockSpec((2, REC_WIDTH), lambda i: (0, 0)),
        ],
        out_specs=[
            pl.BlockSpec((B, D), lambda i: (0, i)),
            pl.BlockSpec((1, REC_WIDTH), lambda i: (0, 0)),
        ],
        out_shape=[jax.ShapeDtypeStruct((B, 3 * D), F32),
                   jax.ShapeDtypeStruct((1, REC_WIDTH), F32)],
        compiler_params=pltpu.CompilerParams(dimension_semantics=("arbitrary",)),
        name="ada_modulation",
    )(c, w_ada[0], b_ada, lower_bounds)

    bd, mrel2, prev_bias, cur_bias, phase, ones = _constants()
    inv_freq = ROPE_THETA ** (-jnp.arange(0, HEAD_DIM, 2, dtype=F32) / HEAD_DIM)
    invf = jnp.tile(inv_freq, LANES // (HEAD_DIM // 2))[None, :]
    qw = jnp.tile(q_norm_w[0], 256 // HEAD_DIM)[None, :] * (HEAD_DIM ** -0.5)
    kw = jnp.tile(k_norm_w[0], 256 // HEAD_DIM)[None, :]
    sink_bias = jnp.where(jnp.arange(BLK)[None, None, None, :] == 0,
                          sinks[0].astype(F32)[None, :, None, None],
                          jnp.asarray(prev_bias)[:, None, :, :])

    const = lambda shape: pl.BlockSpec(shape, lambda b, t: (0,) * len(shape))
    resident = lambda shape: pl.BlockSpec(shape, lambda b, t: (0,) * len(shape),
                                          pipeline_mode=pl.Buffered(1))
    kv_rows = (2 * (T // BLK) + 1) * UNIT
    out = pl.pallas_call(
        _layer_kernel,
        grid=(B, S // T),
        in_specs=[
            pl.BlockSpec((1, T, D), lambda b, t: (b, t, 0)),
            pl.BlockSpec((1, T, 1), lambda b, t: (b, t, 0)),
            pl.BlockSpec((1, 1, 3 * D), lambda b, t: (b, 0, 0)),
            const((1, D)),
            resident((D, IN_WIDTH)),
            resident((ATTN_WIDTH + REC_WIDTH, D)),
            const((1, 256)), const((1, 256)), const((1, REC_DIM)),
            const((1, REC_WIDTH)),
            const((1, LANES)), const((1, LANES)),
            const((256, 256)), const((BLK, 2 * BLK)),
            resident((2, ATTN_HEADS, BLK, BLK)),
            const((BLK, BLK)),
            const((UNIT, LANES)),
        ],
        out_specs=pl.BlockSpec((1, T, D), lambda b, t: (b, t, 0)),
        out_shape=jax.ShapeDtypeStruct((B, S, D), x.dtype),
        scratch_shapes=[
            pltpu.VMEM((T, D), BF16),
            pltpu.VMEM((T, LANES), F32),
            pltpu.VMEM((T, LANES), F32),
            pltpu.VMEM((T, ATTN_WIDTH), BF16),
            pltpu.VMEM((ATTN_KV_HEADS, kv_rows, LANES), BF16),
            pltpu.VMEM((ATTN_KV_HEADS, kv_rows, 2 * LANES), BF16),
            pltpu.VMEM((T, ATTN_WIDTH), BF16),
            pltpu.VMEM((T, REC_WIDTH), F32),
            pltpu.VMEM((T, REC_WIDTH), F32),
            pltpu.VMEM((T, REC_WIDTH), BF16),
            pltpu.VMEM((T, REC_WIDTH), BF16),
            pltpu.VMEM((REC_HEADS, REC_DIM, REC_DIM), F32),
            pltpu.VMEM((T, ATTN_WIDTH + REC_WIDTH), BF16),
        ],
        compiler_params=pltpu.CompilerParams(
            dimension_semantics=("arbitrary", "arbitrary"),
            vmem_limit_bytes=VMEM_LIMIT_BYTES),
        name="hybrid_layer",
    )(x, positions[:, :, None], mod[:, None, :], norm_w, w_in[0].astype(BF16),
      w_out[0].astype(BF16), qw, kw, rec_norm_w, lb, invf, jnp.asarray(phase),
      jnp.asarray(bd, BF16), jnp.asarray(mrel2, BF16), sink_bias, jnp.asarray(cur_bias),
      jnp.asarray(ones, BF16))
    return out
```

```python
import math

import jax
import jax.numpy as jnp
import numpy as np
from jax import lax
from jax.experimental import pallas as pl
from jax.experimental.pallas import tpu as pltpu

F32 = jnp.float32
BF16 = jnp.bfloat16

D_MODEL = 1024
ATTN_HEADS = 16
ATTN_KV_HEADS = 4
HEAD_DIM = 64
KV_WIDTH = ATTN_KV_HEADS * HEAD_DIM
ATTN_WIDTH = ATTN_HEADS * HEAD_DIM
WINDOW = 128
ROPE_THETA = 10000.0
REC_HEADS = 8
REC_DIM = 128
REC_WIDTH = REC_HEADS * REC_DIM
EPS = 1e-6
BLK = 128
LANES = 128
UNIT = 2 * BLK
CHUNK = 256
PROJ_CHUNK = 512

OFF_AQ = 0
OFF_AK = OFF_AQ + ATTN_WIDTH
OFF_AV = OFF_AK + KV_WIDTH
OFF_AG = OFF_AV + KV_WIDTH
OFF_RQ = OFF_AG + ATTN_WIDTH
OFF_RF = OFF_RQ + REC_WIDTH
OFF_RI = OFF_RF + REC_WIDTH
OFF_RG = OFF_RI + REC_WIDTH
IN_WIDTH = OFF_RG + REC_WIDTH

TOKEN_TILE = 512
VMEM_LIMIT_BYTES = 58 * 1024 * 1024

_NT = (((1,), (1,)), ((), ()))


def _share(slot, n_items, n_slots):
    return range(-(-slot * n_items // n_slots), -(-(slot + 1) * n_items // n_slots))


def _sigmoid(x):
    return 0.5 * jnp.tanh(0.5 * x) + 0.5


def _silu(x):
    hx = 0.5 * x
    return hx * jnp.tanh(hx) + hx


def _ada_kernel(c_ref, w_ref, b_ref, lbin_ref, mod_ref, lb_ref):
    mod_ref[...] = jnp.dot(_silu(c_ref[...]).astype(BF16), w_ref[...].astype(BF16),
                           preferred_element_type=F32) + b_ref[...]
    lbv = lbin_ref[...]
    e = jnp.exp(lbv - jnp.max(lbv, axis=0, keepdims=True))
    lb_ref[...] = e[0:1, :] / jnp.sum(e, axis=0, keepdims=True)


def _layer_kernel(x_ref, pos_ref, mod_ref, normw_ref, win_ref, wout_ref,
                  qw_ref, kw_ref, recw_ref, lb_ref, invf_ref, phase_ref, bd_ref, mrel_ref,
                  sinkb_ref, curb_ref, ones_ref, o_ref,
                  hb_s, cos_s, sin_s, q_s, kx_s, vx_s, ga_s,
                  rq_s, rf_s, vr_s, gr_s, st_s, mix_s, ya_s):
    T = x_ref.shape[1]
    nblk = T // BLK
    t = pl.program_id(1)
    mod = mod_ref[0]

    x = x_ref[0]
    scale = mod[:, D_MODEL:2 * D_MODEL]
    ms = jnp.mean(x * x, axis=-1, keepdims=True)
    h = (x * lax.rsqrt(ms + EPS)) * (normw_ref[...] * (1.0 + scale)) + mod[:, 0:D_MODEL]
    hb_s[...] = h.astype(BF16)

    def proj(c0, width=PROJ_CHUNK):
        return jnp.dot(hb_s[...], win_ref[:, c0:c0 + width], preferred_element_type=F32)

    def head_ssq(raw):
        return jnp.dot((raw * raw).astype(BF16), bd_ref[...], preferred_element_type=F32)

    lane = lax.broadcasted_iota(jnp.int32, (T, LANES), 1)
    first_half = (lane & (HEAD_DIM // 2)) == 0
    low_head = lane < HEAD_DIM
    row = lax.broadcasted_iota(jnp.int32, (T, LANES), 0)
    keep_prev = (row & (BLK - 1)) != 0

    def rope(v):
        partner = jnp.where(first_half, pltpu.roll(v, LANES - HEAD_DIM // 2, 1),
                            pltpu.roll(v, HEAD_DIM // 2, 1))
        return v * cos_s[...] + partner * sin_s[...]

    def rope_tables(r0, r1):
        fh = (lax.broadcasted_iota(jnp.int32, (r1 - r0, LANES), 1) & (HEAD_DIM // 2)) == 0
        trig = jnp.sin(pos_ref[0, r0:r1, :].astype(F32) * invf_ref[...] + phase_ref[...])
        cos_s[r0:r1, :] = jnp.where(fh, trig, pltpu.roll(trig, HEAD_DIM // 2, 1))
        sin_s[r0:r1, :] = jnp.where(fh, -pltpu.roll(trig, LANES - HEAD_DIM // 2, 1), trig)

    n_qp = ATTN_WIDTH // PROJ_CHUNK
    for pc in range(n_qp):
        raw = proj(OFF_AQ + pc * PROJ_CHUNK)
        rq_s[:, pc * PROJ_CHUNK:(pc + 1) * PROJ_CHUNK] = raw
        rope_tables(pc * T // n_qp, (pc + 1) * T // n_qp)
        for sub in range(PROJ_CHUNK // CHUNK):
            cols = slice(pc * PROJ_CHUNK + sub * CHUNK, pc * PROJ_CHUNK + (sub + 1) * CHUNK)
            rf_s[:, cols] = head_ssq(raw[:, sub * CHUNK:(sub + 1) * CHUNK])
    kv_raw = proj(OFF_AK, 2 * KV_WIDTH)
    k_raw = kv_raw[:, 0:KV_WIDTH]
    v_raw = kv_raw[:, KV_WIDTH:2 * KV_WIDTH]
    k_ssq = head_ssq(k_raw)

    @pl.when(t > 0)
    def _():
        kx_s[:, 0:UNIT, :] = kx_s[:, 2 * nblk * UNIT:(2 * nblk + 1) * UNIT, :]
        vx_s[:, 0:UNIT, :] = vx_s[:, 2 * nblk * UNIT:(2 * nblk + 1) * UNIT, :]

    @pl.when(t == 0)
    def _():
        kx_s[:, 0:UNIT, :] = jnp.zeros((ATTN_KV_HEADS, UNIT, LANES), BF16)
        vx_s[:, 0:UNIT, 0:LANES] = jnp.zeros((ATTN_KV_HEADS, UNIT, LANES), BF16)
        st_s[...] = jnp.zeros_like(st_s)
        for head in range(ATTN_KV_HEADS):
            for u in range(2 * nblk + 1):
                vx_s[head, u * UNIT:(u + 1) * UNIT, LANES:2 * LANES] = ones_ref[...]

    def store_split(dst_s, pair, kv_pair):
        sw = pltpu.roll(pair, HEAD_DIM, 1)
        zero = jnp.zeros_like(pair)
        parts = ((2 * kv_pair, jnp.where(low_head, pair, zero), jnp.where(low_head, zero, sw)),
                 (2 * kv_pair + 1, jnp.where(low_head, sw, zero), jnp.where(low_head, zero, pair)))
        for head, lo, hi in parts:
            for role_prev in (False, True):
                lo_r = (jnp.where(keep_prev, lo, zero) if role_prev else lo).astype(BF16)
                hi_r = (jnp.where(keep_prev, hi, zero) if role_prev else hi).astype(BF16)
                for blk in range(nblk):
                    base = (2 * blk + (2 if role_prev else 1)) * UNIT
                    dst_s[head, base:base + BLK, 0:LANES] = lo_r[blk * BLK:(blk + 1) * BLK, :]
                    dst_s[head, base + BLK:base + UNIT, 0:LANES] = hi_r[blk * BLK:(blk + 1) * BLK, :]

    def q_epilogue(cb):
        cols = slice(cb * CHUNK, (cb + 1) * CHUNK)
        qn = rq_s[:, cols] * lax.rsqrt(rf_s[:, cols] * (1.0 / HEAD_DIM) + EPS) * qw_ref[...]
        for half in range(CHUNK // LANES):
            q_s[:, cb * CHUNK + half * LANES:cb * CHUNK + (half + 1) * LANES] = rope(
                qn[:, half * LANES:(half + 1) * LANES]).astype(BF16)

    def k_epilogue():
        kn = k_raw * lax.rsqrt(k_ssq * (1.0 / HEAD_DIM) + EPS) * kw_ref[...]
        for half in range(KV_WIDTH // LANES):
            store_split(kx_s, rope(kn[:, half * LANES:(half + 1) * LANES]), half)

    def v_epilogue():
        for half in range(KV_WIDTH // LANES):
            store_split(vx_s, v_raw[:, half * LANES:(half + 1) * LANES], half)

    side_work = [lambda: (q_epilogue(0), k_epilogue(), q_epilogue(1)),
                 lambda: (q_epilogue(2), v_epilogue(), q_epilogue(3))]
    for pc in range(ATTN_WIDTH // PROJ_CHUNK):
        cols = slice(pc * PROJ_CHUNK, (pc + 1) * PROJ_CHUNK)
        ag = proj(OFF_AG + pc * PROJ_CHUNK)
        side_work[pc]()
        ga_s[:, cols] = _silu(ag).astype(BF16)

    def att_scores(blk, j):
        rows = slice(blk * BLK, (blk + 1) * BLK)
        win_rows = slice(2 * blk * UNIT, (2 * blk + 2) * UNIT)
        p2 = jnp.concatenate(
            [q_s[rows, (2 * j) * LANES:(2 * j + 1) * LANES],
             q_s[rows, (2 * j + 1) * LANES:(2 * j + 2) * LANES]], axis=0)
        return lax.dot_general(p2, kx_s[j, win_rows, :], _NT, preferred_element_type=F32)

    def att_softmax_pv(blk, j, s2):
        win_rows = slice(2 * blk * UNIT, (2 * blk + 2) * UNIT)
        p_rows = []
        for r in range(2):
            p_prev = []
            p_cur = []
            for e in range(2):
                head = 4 * j + 2 * r + e
                prev_bias = sinkb_ref[0, head]
                if blk == 0:
                    prev_bias = jnp.where(t == 0, sinkb_ref[1, head], prev_bias)
                sp = s2[r * BLK:(r + 1) * BLK, e * BLK:(e + 1) * BLK] + prev_bias
                sc = s2[r * BLK:(r + 1) * BLK, (2 + e) * BLK:(3 + e) * BLK] + curb_ref[...]
                m = jnp.max(jnp.maximum(sp, sc), axis=1, keepdims=True)
                p_prev.append(jnp.exp2(sp - m).astype(BF16))
                p_cur.append(jnp.exp2(sc - m).astype(BF16))
            p_rows.append(jnp.concatenate(p_prev + p_cur, axis=1))
        pm = jnp.concatenate(p_rows, axis=0)
        return jnp.dot(pm, vx_s[j, win_rows, :], preferred_element_type=F32)

    def att_epilogue(blk, j, o2):
        rows = slice(blk * BLK, (blk + 1) * BLK)
        for r in range(2):
            cols = slice((2 * j + r) * LANES, (2 * j + r + 1) * LANES)
            num = o2[r * BLK:(r + 1) * BLK, 0:LANES]
            den = o2[r * BLK:(r + 1) * BLK, LANES:2 * LANES]
            mix_s[rows, cols] = (num * (1.0 / den) * ga_s[rows, cols].astype(F32)).astype(BF16)

    def rec_chunk(i):
        n = REC_WIDTH // PROJ_CHUNK
        which, pc = divmod(i, n)
        cols = slice(pc * PROJ_CHUNK, (pc + 1) * PROJ_CHUNK)
        if which == 0:
            rq_s[:, cols] = proj(OFF_RQ + pc * PROJ_CHUNK)
        elif which == 1:
            rf_s[:, cols] = proj(OFF_RF + pc * PROJ_CHUNK)
        elif which == 2:
            vr_s[:, cols] = proj(OFF_RI + pc * PROJ_CHUNK).astype(BF16)
        else:
            gr_s[:, cols] = _silu(proj(OFF_RG + pc * PROJ_CHUNK)).astype(BF16)

    units = [(blk, j) for blk in range(nblk) for j in range(ATTN_KV_HEADS)]
    n_rec_chunks = 4 * (REC_WIDTH // PROJ_CHUNK)
    scores = att_scores(*units[0])
    pending = None
    for u, (blk, j) in enumerate(units):
        for i in _share(u, n_rec_chunks, len(units)):
            rec_chunk(i)
        nxt = att_scores(*units[u + 1]) if u + 1 < len(units) else None
        o2 = att_softmax_pv(blk, j, scores)
        if pending is not None:
            att_epilogue(*pending)
        pending = (blk, j, o2)
        scores = nxt
    att_epilogue(*pending)

    row_i = lax.broadcasted_iota(jnp.int32, (BLK, BLK), 0)
    col_i = lax.broadcasted_iota(jnp.int32, (BLK, BLK), 1)
    causal = row_i >= col_i
    lb_row = lb_ref[...]

    def rec_block(blk):
        rows = slice(blk * BLK, (blk + 1) * BLK)
        f = lb_row + (1.0 - lb_row) * _sigmoid(rf_s[rows, :])
        lf = jnp.log(f)
        kk = 1.0 - f
        lf_hi = lf.astype(BF16)
        lf_lo = (lf - lf_hi.astype(F32)).astype(BF16)
        brel = jnp.dot(mrel_ref[...], jnp.concatenate([lf_hi, lf_lo], axis=0),
                       preferred_element_type=F32)
        epos = jnp.exp(brel)
        eneg = jnp.exp(-brel)
        d_mid = f[0:1, :] * eneg[0:1, :]
        d_tail = epos[BLK - 1:BLK, :]
        q_rel = rq_s[rows, :] * epos
        q_abs = (q_rel * d_mid).astype(BF16)
        q_rel = q_rel.astype(BF16)
        k_inv = kk * eneg
        k_out = (k_inv * d_tail).astype(BF16)
        k_inv = k_inv.astype(BF16)
        d_all = d_tail * d_mid

        a_mats = []
        vts = []
        us = []
        for hh in range(REC_HEADS):
            cols = slice(hh * REC_DIM, (hh + 1) * REC_DIM)
            a_mats.append(lax.dot_general(q_rel[:, cols], k_inv[:, cols], _NT,
                                          preferred_element_type=F32))
            vt = vr_s[rows, cols].T
            vts.append(vt)
            us.append(jnp.dot(vt, k_out[:, cols], preferred_element_type=F32))

        outs = []
        for hh in range(REC_HEADS):
            cols = slice(hh * REC_DIM, (hh + 1) * REC_DIM)
            a = jnp.where(causal, a_mats[hh], 0.0).astype(BF16)
            st = st_s[hh]
            outs.append(lax.dot_general(jnp.concatenate([a, q_abs[:, cols]], axis=1),
                                        jnp.concatenate([vts[hh], st.astype(BF16)], axis=1), _NT,
                                        preferred_element_type=F32))
            st_s[hh] = st * d_all[:, cols] + us[hh]

        for hh in range(REC_HEADS):
            cols = slice(hh * REC_DIM, (hh + 1) * REC_DIM)
            o = outs[hh]
            rr = lax.rsqrt(jnp.mean(o * o, axis=1, keepdims=True) + EPS)
            rec = (o * rr) * recw_ref[...] * gr_s[rows, cols].astype(F32)
            mix_s[rows, ATTN_WIDTH + hh * REC_DIM:ATTN_WIDTH + (hh + 1) * REC_DIM] = rec.astype(BF16)

    n_out = D_MODEL // PROJ_CHUNK
    for blk in range(nblk):
        for pc in _share(blk, n_out, nblk):
            oc = slice(pc * PROJ_CHUNK, (pc + 1) * PROJ_CHUNK)
            ya_s[:, oc] = jnp.dot(mix_s[:, 0:ATTN_WIDTH], wout_ref[0:ATTN_WIDTH, oc],
                                  preferred_element_type=F32)
        rec_block(blk)

    y = ya_s[...] + jnp.dot(mix_s[:, ATTN_WIDTH:ATTN_WIDTH + REC_WIDTH],
                            wout_ref[ATTN_WIDTH:ATTN_WIDTH + REC_WIDTH, :],
                            preferred_element_type=F32)
    o_ref[0] = x_ref[0] + mod[:, 2 * D_MODEL:3 * D_MODEL] * y


def _constants():
    head_of = np.arange(CHUNK) // HEAD_DIM
    bd = (head_of[:, None] == head_of[None, :]).astype(np.float32)
    mid = BLK // 2 - 1
    tt = np.arange(BLK)[:, None]
    rr = np.arange(BLK)[None, :]
    mrel = ((rr <= tt).astype(np.float32) - (rr <= mid).astype(np.float32))
    mrel2 = np.concatenate([mrel, mrel], axis=1)
    assert WINDOW == BLK
    qi = np.arange(BLK)[:, None]
    kj = np.arange(BLK)[None, :]
    ninf = np.float32(-np.inf)
    prev = np.where(kj > qi, 0.0, ninf).astype(np.float32)
    cur = np.where(kj <= qi, 0.0, ninf).astype(np.float32)
    none = np.full((BLK, BLK), ninf, np.float32)
    lane = np.arange(LANES)
    phase = np.where((lane % HEAD_DIM) < HEAD_DIM // 2, np.pi / 2, 0.0).astype(np.float32)[None, :]
    ones = np.zeros((UNIT, LANES), np.float32)
    ones[0:BLK, 0:HEAD_DIM] = 1.0
    ones[BLK:UNIT, HEAD_DIM:LANES] = 1.0
    return bd, mrel2, np.stack([prev, none]), cur, phase, ones


def kernel(x, c, positions, norm_w, w_ada, b_ada, w_in, q_norm_w, k_norm_w, sinks, rec_norm_w,
           lower_bounds, w_out):
    B, S, D = x.shape
    assert D == D_MODEL and w_in.shape == (1, D_MODEL, IN_WIDTH) and S % TOKEN_TILE == 0
    assert lower_bounds.shape[0] == 2 and w_out.shape == (1, ATTN_WIDTH + REC_WIDTH, D_MODEL)
    T = TOKEN_TILE

    n_ada = 3
    mod, lb = pl.pallas_call(
        _ada_kernel,
        grid=(n_ada,),
        in_specs=[
            pl.BlockSpec((B, D), lambda i: (0, 0)),
            pl.BlockSpec((D, D), lambda i: (0, i)),
            pl.BlockSpec((1, D), lambda i: (0, i)),
            pl.BlockSpec((2, REC_WIDTH), lambda i: (0, 0)),
        ],
        out_specs=[
            pl.BlockSpec((B, D), lambda i: (0, i)),
            pl.BlockSpec((1, REC_WIDTH), lambda i: (0, 0)),
        ],
        out_shape=[jax.ShapeDtypeStruct((B, 3 * D), F32),
                   jax.ShapeDtypeStruct((1, REC_WIDTH), F32)],
        compiler_params=pltpu.CompilerParams(dimension_semantics=("arbitrary",)),
        name="ada_modulation",
    )(c, w_ada[0], b_ada, lower_bounds)

    bd, mrel2, prev_bias, cur_bias, phase, ones = _constants()
    inv_freq = ROPE_THETA ** (-jnp.arange(0, HEAD_DIM, 2, dtype=F32) / HEAD_DIM)
    invf = jnp.tile(inv_freq, LANES // (HEAD_DIM // 2))[None, :]
    log2e = math.log2(math.e)
    qw = jnp.tile(q_norm_w[0], CHUNK // HEAD_DIM)[None, :] * (HEAD_DIM ** -0.5 * log2e)
    kw = jnp.tile(k_norm_w[0], CHUNK // HEAD_DIM)[None, :]
    sink_bias = jnp.where(jnp.arange(BLK)[None, None, None, :] == 0,
                          (sinks[0].astype(F32) * log2e)[None, :, None, None],
                          jnp.asarray(prev_bias)[:, None, :, :])

    const = lambda shape: pl.BlockSpec(shape, lambda b, t: (0,) * len(shape))
    resident = lambda shape: pl.BlockSpec(shape, lambda b, t: (0,) * len(shape),
                                          pipeline_mode=pl.Buffered(1))
    kv_rows = (2 * (T // BLK) + 1) * UNIT
    out = pl.pallas_call(
        _layer_kernel,
        grid=(B, S // T),
        in_specs=[
            pl.BlockSpec((1, T, D), lambda b, t: (b, t, 0)),
            pl.BlockSpec((1, T, 1), lambda b, t: (b, t, 0)),
            pl.BlockSpec((1, 1, 3 * D), lambda b, t: (b, 0, 0)),
            const((1, D)),
            resident((D, IN_WIDTH)),
            resident((ATTN_WIDTH + REC_WIDTH, D)),
            const((1, CHUNK)), const((1, CHUNK)), const((1, REC_DIM)),
            const((1, REC_WIDTH)),
            const((1, LANES)), const((1, LANES)),
            const((CHUNK, CHUNK)), const((BLK, 2 * BLK)),
            resident((2, ATTN_HEADS, BLK, BLK)),
            const((BLK, BLK)),
            const((UNIT, LANES)),
        ],
        out_specs=pl.BlockSpec((1, T, D), lambda b, t: (b, t, 0)),
        out_shape=jax.ShapeDtypeStruct((B, S, D), x.dtype),
        scratch_shapes=[
            pltpu.VMEM((T, D), BF16),
            pltpu.VMEM((T, LANES), F32),
            pltpu.VMEM((T, LANES), F32),
            pltpu.VMEM((T, ATTN_WIDTH), BF16),
            pltpu.VMEM((ATTN_KV_HEADS, kv_rows, LANES), BF16),
            pltpu.VMEM((ATTN_KV_HEADS, kv_rows, 2 * LANES), BF16),
            pltpu.VMEM((T, ATTN_WIDTH), BF16),
            pltpu.VMEM((T, REC_WIDTH), F32),
            pltpu.VMEM((T, REC_WIDTH), F32),
            pltpu.VMEM((T, REC_WIDTH), BF16),
            pltpu.VMEM((T, REC_WIDTH), BF16),
            pltpu.VMEM((REC_HEADS, REC_DIM, REC_DIM), F32),
            pltpu.VMEM((T, ATTN_WIDTH + REC_WIDTH), BF16),
            pltpu.VMEM((T, D), F32),
        ],
        compiler_params=pltpu.CompilerParams(
            dimension_semantics=("arbitrary", "arbitrary"),
            vmem_limit_bytes=VMEM_LIMIT_BYTES),
        name="hybrid_layer",
    )(x, positions[:, :, None], mod[:, None, :], norm_w, w_in[0].astype(BF16),
      w_out[0].astype(BF16), qw, kw, rec_norm_w, lb, invf, jnp.asarray(phase),
      jnp.asarray(bd, BF16), jnp.asarray(mrel2, BF16), sink_bias, jnp.asarray(cur_bias),
      jnp.asarray(ones, BF16))
    return out
```

```python
import math

import jax
import jax.numpy as jnp
import numpy as np
from jax import lax
from jax.experimental import pallas as pl
from jax.experimental.pallas import tpu as pltpu

F32 = jnp.float32
BF16 = jnp.bfloat16

D_MODEL = 1024
ATTN_HEADS = 16
ATTN_KV_HEADS = 4
HEAD_DIM = 64
KV_WIDTH = ATTN_KV_HEADS * HEAD_DIM
ATTN_WIDTH = ATTN_HEADS * HEAD_DIM
WINDOW = 128
ROPE_THETA = 10000.0
REC_HEADS = 8
REC_DIM = 128
REC_WIDTH = REC_HEADS * REC_DIM
EPS = 1e-6
BLK = 128
LANES = 128
UNIT = 2 * BLK
CHUNK = 256
PROJ_CHUNK = 512

OFF_AQ = 0
OFF_AK = OFF_AQ + ATTN_WIDTH
OFF_AV = OFF_AK + KV_WIDTH
OFF_AG = OFF_AV + KV_WIDTH
OFF_RQ = OFF_AG + ATTN_WIDTH
OFF_RF = OFF_RQ + REC_WIDTH
OFF_RI = OFF_RF + REC_WIDTH
OFF_RG = OFF_RI + REC_WIDTH
IN_WIDTH = OFF_RG + REC_WIDTH

TOKEN_TILE = 512
VMEM_LIMIT_BYTES = 58 * 1024 * 1024

_NT = (((1,), (1,)), ((), ()))


def _share(slot, n_items, n_slots):
    return range(-(-slot * n_items // n_slots), -(-(slot + 1) * n_items // n_slots))


def _sigmoid(x):
    return 0.5 * jnp.tanh(0.5 * x) + 0.5


def _silu(x):
    hx = 0.5 * x
    return hx * jnp.tanh(hx) + hx


def _ada_kernel(c_ref, w_ref, b_ref, lbin_ref, mod_ref, lb_ref):
    mod_ref[...] = jnp.dot(_silu(c_ref[...]).astype(BF16), w_ref[...].astype(BF16),
                           preferred_element_type=F32) + b_ref[...]
    lbv = lbin_ref[...]
    e = jnp.exp(lbv - jnp.max(lbv, axis=0, keepdims=True))
    lb_ref[...] = e[0:1, :] / jnp.sum(e, axis=0, keepdims=True)


def _layer_kernel(x_ref, pos_ref, mod_ref, normw_ref, win_ref, wout_ref,
                  qw_ref, kw_ref, recw_ref, lb_ref, invf_ref, phase_ref, bd_ref, mrel_ref,
                  sinkb_ref, curb_ref, ones_ref, o_ref,
                  hb_s, cos_s, sin_s, q_s, kx_s, vx_s, ga_s,
                  rq_s, rf_s, vr_s, gr_s, st_s, mix_s, ya_s):
    T = x_ref.shape[1]
    nblk = T // BLK
    t = pl.program_id(1)
    mod = mod_ref[0]

    @pl.when(t > 0)
    def _():
        kx_s[:, 0:UNIT, :] = kx_s[:, 2 * nblk * UNIT:(2 * nblk + 1) * UNIT, :]
        vx_s[:, 0:UNIT, :] = vx_s[:, 2 * nblk * UNIT:(2 * nblk + 1) * UNIT, :]

    @pl.when(t == 0)
    def _():
        kx_s[:, 0:UNIT, :] = jnp.zeros((ATTN_KV_HEADS, UNIT, LANES), BF16)
        vx_s[:, 0:UNIT, 0:LANES] = jnp.zeros((ATTN_KV_HEADS, UNIT, LANES), BF16)
        st_s[...] = jnp.zeros_like(st_s)
        for head in range(ATTN_KV_HEADS):
            for u in range(2 * nblk + 1):
                vx_s[head, u * UNIT:(u + 1) * UNIT, LANES:2 * LANES] = ones_ref[...]

    x = x_ref[0]
    scale = mod[:, D_MODEL:2 * D_MODEL]
    ms = jnp.mean(x * x, axis=-1, keepdims=True)
    h = (x * lax.rsqrt(ms + EPS)) * (normw_ref[...] * (1.0 + scale)) + mod[:, 0:D_MODEL]
    hb_s[...] = h.astype(BF16)

    def proj(c0, width=PROJ_CHUNK):
        return jnp.dot(hb_s[...], win_ref[:, c0:c0 + width], preferred_element_type=F32)

    def head_ssq(raw):
        return jnp.dot((raw * raw).astype(BF16), bd_ref[...], preferred_element_type=F32)

    lane = lax.broadcasted_iota(jnp.int32, (T, LANES), 1)
    first_half = (lane & (HEAD_DIM // 2)) == 0
    low_head = lane < HEAD_DIM
    row = lax.broadcasted_iota(jnp.int32, (T, LANES), 0)
    keep_prev = (row & (BLK - 1)) != 0

    def rope(v):
        partner = jnp.where(first_half, pltpu.roll(v, LANES - HEAD_DIM // 2, 1),
                            pltpu.roll(v, HEAD_DIM // 2, 1))
        return v * cos_s[...] + partner * sin_s[...]

    def rope_tables(r0, r1):
        fh = (lax.broadcasted_iota(jnp.int32, (r1 - r0, LANES), 1) & (HEAD_DIM // 2)) == 0
        trig = jnp.sin(pos_ref[0, r0:r1, :].astype(F32) * invf_ref[...] + phase_ref[...])
        cos_s[r0:r1, :] = jnp.where(fh, trig, pltpu.roll(trig, HEAD_DIM // 2, 1))
        sin_s[r0:r1, :] = jnp.where(fh, -pltpu.roll(trig, LANES - HEAD_DIM // 2, 1), trig)

    n_qp = ATTN_WIDTH // PROJ_CHUNK
    n_tab = 2 * n_qp
    for pc in range(n_qp):
        raw = proj(OFF_AQ + pc * PROJ_CHUNK)
        rq_s[:, pc * PROJ_CHUNK:(pc + 1) * PROJ_CHUNK] = raw
        rope_tables(2 * pc * T // n_tab, (2 * pc + 1) * T // n_tab)
        for sub in range(PROJ_CHUNK // CHUNK):
            cols = slice(pc * PROJ_CHUNK + sub * CHUNK, pc * PROJ_CHUNK + (sub + 1) * CHUNK)
            rf_s[:, cols] = head_ssq(raw[:, sub * CHUNK:(sub + 1) * CHUNK])
        rope_tables((2 * pc + 1) * T // n_tab, (2 * pc + 2) * T // n_tab)
    kv_raw = proj(OFF_AK, 2 * KV_WIDTH)
    k_raw = kv_raw[:, 0:KV_WIDTH]
    v_raw = kv_raw[:, KV_WIDTH:2 * KV_WIDTH]
    k_ssq = head_ssq(k_raw)

    def store_split(dst_s, pair, kv_pair):
        sw = pltpu.roll(pair, HEAD_DIM, 1)
        zero = jnp.zeros_like(pair)
        parts = ((2 * kv_pair, jnp.where(low_head, pair, zero), jnp.where(low_head, zero, sw)),
                 (2 * kv_pair + 1, jnp.where(low_head, sw, zero), jnp.where(low_head, zero, pair)))
        for head, lo, hi in parts:
            for role_prev in (False, True):
                lo_r = (jnp.where(keep_prev, lo, zero) if role_prev else lo).astype(BF16)
                hi_r = (jnp.where(keep_prev, hi, zero) if role_prev else hi).astype(BF16)
                for blk in range(nblk):
                    base = (2 * blk + (2 if role_prev else 1)) * UNIT
                    dst_s[head, base:base + BLK, 0:LANES] = lo_r[blk * BLK:(blk + 1) * BLK, :]
                    dst_s[head, base + BLK:base + UNIT, 0:LANES] = hi_r[blk * BLK:(blk + 1) * BLK, :]

    def q_epilogue(cb):
        cols = slice(cb * CHUNK, (cb + 1) * CHUNK)
        qn = rq_s[:, cols] * lax.rsqrt(rf_s[:, cols] * (1.0 / HEAD_DIM) + EPS) * qw_ref[...]
        for half in range(CHUNK // LANES):
            q_s[:, cb * CHUNK + half * LANES:cb * CHUNK + (half + 1) * LANES] = rope(
                qn[:, half * LANES:(half + 1) * LANES]).astype(BF16)

    def k_epilogue():
        kn = k_raw * lax.rsqrt(k_ssq * (1.0 / HEAD_DIM) + EPS) * kw_ref[...]
        for half in range(KV_WIDTH // LANES):
            store_split(kx_s, rope(kn[:, half * LANES:(half + 1) * LANES]), half)

    def v_epilogue():
        for half in range(KV_WIDTH // LANES):
            store_split(vx_s, v_raw[:, half * LANES:(half + 1) * LANES], half)

    side_work = [lambda: (q_epilogue(0), k_epilogue(), q_epilogue(1)),
                 lambda: (q_epilogue(2), v_epilogue(), q_epilogue(3))]
    for pc in range(ATTN_WIDTH // PROJ_CHUNK):
        cols = slice(pc * PROJ_CHUNK, (pc + 1) * PROJ_CHUNK)
        ag = proj(OFF_AG + pc * PROJ_CHUNK)
        side_work[pc]()
        ga_s[:, cols] = _silu(ag).astype(BF16)

    def att_scores(blk, j):
        rows = slice(blk * BLK, (blk + 1) * BLK)
        win_rows = slice(2 * blk * UNIT, (2 * blk + 2) * UNIT)
        p2 = jnp.concatenate(
            [q_s[rows, (2 * j) * LANES:(2 * j + 1) * LANES],
             q_s[rows, (2 * j + 1) * LANES:(2 * j + 2) * LANES]], axis=0)
        return lax.dot_general(p2, kx_s[j, win_rows, :], _NT, preferred_element_type=F32)

    def att_softmax_pv(blk, j, s2):
        win_rows = slice(2 * blk * UNIT, (2 * blk + 2) * UNIT)
        p_rows = []
        for r in range(2):
            p_prev = []
            p_cur = []
            for e in range(2):
                head = 4 * j + 2 * r + e
                prev_bias = sinkb_ref[0, head]
                if blk == 0:
                    prev_bias = jnp.where(t == 0, sinkb_ref[1, head], prev_bias)
                sp = s2[r * BLK:(r + 1) * BLK, e * BLK:(e + 1) * BLK] + prev_bias
                sc = s2[r * BLK:(r + 1) * BLK, (2 + e) * BLK:(3 + e) * BLK] + curb_ref[...]
                m = jnp.max(jnp.maximum(sp, sc), axis=1, keepdims=True)
                p_prev.append(jnp.exp2(sp - m).astype(BF16))
                p_cur.append(jnp.exp2(sc - m).astype(BF16))
            p_rows.append(jnp.concatenate(p_prev + p_cur, axis=1))
        pm = jnp.concatenate(p_rows, axis=0)
        return jnp.dot(pm, vx_s[j, win_rows, :], preferred_element_type=F32)

    def att_epilogue(blk, j, o2):
        rows = slice(blk * BLK, (blk + 1) * BLK)
        for r in range(2):
            cols = slice((2 * j + r) * LANES, (2 * j + r + 1) * LANES)
            num = o2[r * BLK:(r + 1) * BLK, 0:LANES]
            den = o2[r * BLK:(r + 1) * BLK, LANES:2 * LANES]
            mix_s[rows, cols] = (num * (1.0 / den) * ga_s[rows, cols].astype(F32)).astype(BF16)

    def rec_chunk(i):
        n = REC_WIDTH // PROJ_CHUNK
        which, pc = divmod(i, n)
        cols = slice(pc * PROJ_CHUNK, (pc + 1) * PROJ_CHUNK)
        if which == 0:
            rq_s[:, cols] = proj(OFF_RQ + pc * PROJ_CHUNK)
        elif which == 1:
            rf_s[:, cols] = proj(OFF_RF + pc * PROJ_CHUNK)
        elif which == 2:
            vr_s[:, cols] = proj(OFF_RI + pc * PROJ_CHUNK).astype(BF16)
        else:
            gr_s[:, cols] = _silu(proj(OFF_RG + pc * PROJ_CHUNK)).astype(BF16)

    units = [(blk, j) for blk in range(nblk) for j in range(ATTN_KV_HEADS)]
    n_rec_chunks = 4 * (REC_WIDTH // PROJ_CHUNK)
    scores = att_scores(*units[0])
    pending = None
    for u, (blk, j) in enumerate(units):
        for i in _share(u, n_rec_chunks, len(units)):
            rec_chunk(i)
        nxt = att_scores(*units[u + 1]) if u + 1 < len(units) else None
        o2 = att_softmax_pv(blk, j, scores)
        if pending is not None:
            att_epilogue(*pending)
        pending = (blk, j, o2)
        scores = nxt
    att_epilogue(*pending)

    row_i = lax.broadcasted_iota(jnp.int32, (BLK, BLK), 0)
    col_i = lax.broadcasted_iota(jnp.int32, (BLK, BLK), 1)
    causal = row_i >= col_i
    lb_row = lb_ref[...]

    def rec_block(blk):
        rows = slice(blk * BLK, (blk + 1) * BLK)
        f = lb_row + (1.0 - lb_row) * _sigmoid(rf_s[rows, :])
        lf = jnp.log2(f)
        kk = 1.0 - f
        lf_hi = lf.astype(BF16)
        lf_lo = (lf - lf_hi.astype(F32)).astype(BF16)
        brel = jnp.dot(mrel_ref[...], jnp.concatenate([lf_hi, lf_lo], axis=0),
                       preferred_element_type=F32)
        epos = jnp.exp2(brel)
        eneg = jnp.exp2(-brel)
        d_mid = f[0:1, :] * eneg[0:1, :]
        d_tail = epos[BLK - 1:BLK, :]
        q_rel = rq_s[rows, :] * epos
        q_abs = (q_rel * d_mid).astype(BF16)
        q_rel = q_rel.astype(BF16)
        k_inv = kk * eneg
        k_out = (k_inv * d_tail).astype(BF16)
        k_inv = k_inv.astype(BF16)
        d_all = d_tail * d_mid

        a_mats = []
        vts = []
        us = []
        for hh in range(REC_HEADS):
            cols = slice(hh * REC_DIM, (hh + 1) * REC_DIM)
            a_mats.append(lax.dot_general(q_rel[:, cols], k_inv[:, cols], _NT,
                                          preferred_element_type=F32))
            vt = vr_s[rows, cols].T
            vts.append(vt)
            us.append(jnp.dot(vt, k_out[:, cols], preferred_element_type=F32))

        outs = []
        for hh in range(REC_HEADS):
            cols = slice(hh * REC_DIM, (hh + 1) * REC_DIM)
            a = jnp.where(causal, a_mats[hh], 0.0).astype(BF16)
            st = st_s[hh]
            outs.append(lax.dot_general(jnp.concatenate([a, q_abs[:, cols]], axis=1),
                                        jnp.concatenate([vts[hh], st.astype(BF16)], axis=1), _NT,
                                        preferred_element_type=F32))
            st_s[hh] = st * d_all[:, cols] + us[hh]

        for hh in range(REC_HEADS):
            cols = slice(hh * REC_DIM, (hh + 1) * REC_DIM)
            o = outs[hh]
            rr = lax.rsqrt(jnp.mean(o * o, axis=1, keepdims=True) + EPS)
            rec = (o * rr) * recw_ref[...] * gr_s[rows, cols].astype(F32)
            mix_s[rows, ATTN_WIDTH + hh * REC_DIM:ATTN_WIDTH + (hh + 1) * REC_DIM] = rec.astype(BF16)

    n_out = D_MODEL // PROJ_CHUNK
    for blk in range(nblk):
        for pc in _share(blk, n_out, nblk):
            oc = slice(pc * PROJ_CHUNK, (pc + 1) * PROJ_CHUNK)
            ya_s[:, oc] = jnp.dot(mix_s[:, 0:ATTN_WIDTH], wout_ref[0:ATTN_WIDTH, oc],
                                  preferred_element_type=F32)
        rec_block(blk)

    y = ya_s[...] + jnp.dot(mix_s[:, ATTN_WIDTH:ATTN_WIDTH + REC_WIDTH],
                            wout_ref[ATTN_WIDTH:ATTN_WIDTH + REC_WIDTH, :],
                            preferred_element_type=F32)
    o_ref[0] = x_ref[0] + mod[:, 2 * D_MODEL:3 * D_MODEL] * y


def _constants():
    head_of = np.arange(CHUNK) // HEAD_DIM
    bd = (head_of[:, None] == head_of[None, :]).astype(np.float32)
    mid = BLK // 2 - 1
    tt = np.arange(BLK)[:, None]
    rr = np.arange(BLK)[None, :]
    mrel = ((rr <= tt).astype(np.float32) - (rr <= mid).astype(np.float32))
    mrel2 = np.concatenate([mrel, mrel], axis=1)
    assert WINDOW == BLK
    qi = np.arange(BLK)[:, None]
    kj = np.arange(BLK)[None, :]
    ninf = np.float32(-np.inf)
    prev = np.where(kj > qi, 0.0, ninf).astype(np.float32)
    cur = np.where(kj <= qi, 0.0, ninf).astype(np.float32)
    none = np.full((BLK, BLK), ninf, np.float32)
    lane = np.arange(LANES)
    phase = np.where((lane % HEAD_DIM) < HEAD_DIM // 2, np.pi / 2, 0.0).astype(np.float32)[None, :]
    ones = np.zeros((UNIT, LANES), np.float32)
    ones[0:BLK, 0:HEAD_DIM] = 1.0
    ones[BLK:UNIT, HEAD_DIM:LANES] = 1.0
    return bd, mrel2, np.stack([prev, none]), cur, phase, ones


def kernel(x, c, positions, norm_w, w_ada, b_ada, w_in, q_norm_w, k_norm_w, sinks, rec_norm_w,
           lower_bounds, w_out):
    B, S, D = x.shape
    assert D == D_MODEL and w_in.shape == (1, D_MODEL, IN_WIDTH) and S % TOKEN_TILE == 0
    assert lower_bounds.shape[0] == 2 and w_out.shape == (1, ATTN_WIDTH + REC_WIDTH, D_MODEL)
    T = TOKEN_TILE

    n_ada = 3
    mod, lb = pl.pallas_call(
        _ada_kernel,
        grid=(n_ada,),
        in_specs=[
            pl.BlockSpec((B, D), lambda i: (0, 0)),
            pl.BlockSpec((D, D), lambda i: (0, i)),
            pl.BlockSpec((1, D), lambda i: (0, i)),
            pl.BlockSpec((2, REC_WIDTH), lambda i: (0, 0)),
        ],
        out_specs=[
            pl.BlockSpec((B, D), lambda i: (0, i)),
            pl.BlockSpec((1, REC_WIDTH), lambda i: (0, 0)),
        ],
        out_shape=[jax.ShapeDtypeStruct((B, 3 * D), F32),
                   jax.ShapeDtypeStruct((1, REC_WIDTH), F32)],
        compiler_params=pltpu.CompilerParams(dimension_semantics=("arbitrary",)),
        name="ada_modulation",
    )(c, w_ada[0], b_ada, lower_bounds)

    bd, mrel2, prev_bias, cur_bias, phase, ones = _constants()
    inv_freq = ROPE_THETA ** (-jnp.arange(0, HEAD_DIM, 2, dtype=F32) / HEAD_DIM)
    invf = jnp.tile(inv_freq, LANES // (HEAD_DIM // 2))[None, :]
    log2e = math.log2(math.e)
    qw = jnp.tile(q_norm_w[0], CHUNK // HEAD_DIM)[None, :] * (HEAD_DIM ** -0.5 * log2e)
    kw = jnp.tile(k_norm_w[0], CHUNK // HEAD_DIM)[None, :]
    sink_bias = jnp.where(jnp.arange(BLK)[None, None, None, :] == 0,
                          (sinks[0].astype(F32) * log2e)[None, :, None, None],
                          jnp.asarray(prev_bias)[:, None, :, :])

    const = lambda shape: pl.BlockSpec(shape, lambda b, t: (0,) * len(shape))
    resident = lambda shape: pl.BlockSpec(shape, lambda b, t: (0,) * len(shape),
                                          pipeline_mode=pl.Buffered(1))
    kv_rows = (2 * (T // BLK) + 1) * UNIT
    out = pl.pallas_call(
        _layer_kernel,
        grid=(B, S // T),
        in_specs=[
            pl.BlockSpec((1, T, D), lambda b, t: (b, t, 0)),
            pl.BlockSpec((1, T, 1), lambda b, t: (b, t, 0)),
            pl.BlockSpec((1, 1, 3 * D), lambda b, t: (b, 0, 0)),
            const((1, D)),
            resident((D, IN_WIDTH)),
            resident((ATTN_WIDTH + REC_WIDTH, D)),
            const((1, CHUNK)), const((1, CHUNK)), const((1, REC_DIM)),
            const((1, REC_WIDTH)),
            const((1, LANES)), const((1, LANES)),
            const((CHUNK, CHUNK)), const((BLK, 2 * BLK)),
            resident((2, ATTN_HEADS, BLK, BLK)),
            const((BLK, BLK)),
            const((UNIT, LANES)),
        ],
        out_specs=pl.BlockSpec((1, T, D), lambda b, t: (b, t, 0)),
        out_shape=jax.ShapeDtypeStruct((B, S, D), x.dtype),
        scratch_shapes=[
            pltpu.VMEM((T, D), BF16),
            pltpu.VMEM((T, LANES), F32),
            pltpu.VMEM((T, LANES), F32),
            pltpu.VMEM((T, ATTN_WIDTH), BF16),
            pltpu.VMEM((ATTN_KV_HEADS, kv_rows, LANES), BF16),
            pltpu.VMEM((ATTN_KV_HEADS, kv_rows, 2 * LANES), BF16),
            pltpu.VMEM((T, ATTN_WIDTH), BF16),
            pltpu.VMEM((T, REC_WIDTH), F32),
            pltpu.VMEM((T, REC_WIDTH), F32),
            pltpu.VMEM((T, REC_WIDTH), BF16),
            pltpu.VMEM((T, REC_WIDTH), BF16),
            pltpu.VMEM((REC_HEADS, REC_DIM, REC_DIM), F32),
            pltpu.VMEM((T, ATTN_WIDTH + REC_WIDTH), BF16),
            pltpu.VMEM((T, D), F32),
        ],
        compiler_params=pltpu.CompilerParams(
            dimension_semantics=("arbitrary", "arbitrary"),
            vmem_limit_bytes=VMEM_LIMIT_BYTES),
        name="hybrid_layer",
    )(x, positions[:, :, None], mod[:, None, :], norm_w, w_in[0].astype(BF16),
      w_out[0].astype(BF16), qw, kw, rec_norm_w, lb, invf, jnp.asarray(phase),
      jnp.asarray(bd, BF16), jnp.asarray(mrel2, BF16), sink_bias, jnp.asarray(cur_bias),
      jnp.asarray(ones, BF16))
    return out
```

```python
import math

import jax
import jax.numpy as jnp
import numpy as np
from jax import lax
from jax.experimental import pallas as pl
from jax.experimental.pallas import tpu as pltpu

F32 = jnp.float32
BF16 = jnp.bfloat16

D_MODEL = 1024
ATTN_HEADS = 16
ATTN_KV_HEADS = 4
HEAD_DIM = 64
KV_WIDTH = ATTN_KV_HEADS * HEAD_DIM
ATTN_WIDTH = ATTN_HEADS * HEAD_DIM
WINDOW = 128
ROPE_THETA = 10000.0
REC_HEADS = 8
REC_DIM = 128
REC_WIDTH = REC_HEADS * REC_DIM
EPS = 1e-6
BLK = 128
LANES = 128
UNIT = 2 * BLK
CHUNK = 256
PROJ_CHUNK = 512

OFF_AQ = 0
OFF_AK = OFF_AQ + ATTN_WIDTH
OFF_AV = OFF_AK + KV_WIDTH
OFF_AG = OFF_AV + KV_WIDTH
OFF_RQ = OFF_AG + ATTN_WIDTH
OFF_RF = OFF_RQ + REC_WIDTH
OFF_RI = OFF_RF + REC_WIDTH
OFF_RG = OFF_RI + REC_WIDTH
IN_WIDTH = OFF_RG + REC_WIDTH

TOKEN_TILE = 512
VMEM_LIMIT_BYTES = 58 * 1024 * 1024

_NT = (((1,), (1,)), ((), ()))


def _share(slot, n_items, n_slots):
    return range(-(-slot * n_items // n_slots), -(-(slot + 1) * n_items // n_slots))


def _sigmoid(x):
    return 0.5 * jnp.tanh(0.5 * x) + 0.5


def _silu(x):
    hx = 0.5 * x
    return hx * jnp.tanh(hx) + hx


def _ada_kernel(c_ref, w_ref, b_ref, lbin_ref, mod_ref, lb_ref):
    mod_ref[...] = jnp.dot(_silu(c_ref[...]).astype(BF16), w_ref[...].astype(BF16),
                           preferred_element_type=F32) + b_ref[...]
    lbv = lbin_ref[...]
    e = jnp.exp(lbv - jnp.max(lbv, axis=0, keepdims=True))
    lb_ref[...] = e[0:1, :] / jnp.sum(e, axis=0, keepdims=True)


def _layer_kernel(x_ref, pos_ref, mod_ref, normw_ref, win_ref, wout_ref,
                  qw_ref, kw_ref, recw_ref, lb_ref, invf_ref, phase_ref, bd_ref, mrel_ref,
                  sinkb_ref, curb_ref, ones_ref, o_ref,
                  hb_s, cos_s, sin_s, q_s, kx_s, vx_s, ga_s,
                  rq_s, rf_s, vr_s, gr_s, st_s, mix_s, ya_s):
    T = x_ref.shape[1]
    nblk = T // BLK
    t = pl.program_id(1)
    mod = mod_ref[0]

    @pl.when(t > 0)
    def _():
        kx_s[:, 0:UNIT, :] = kx_s[:, 2 * nblk * UNIT:(2 * nblk + 1) * UNIT, :]
        vx_s[:, 0:UNIT, :] = vx_s[:, 2 * nblk * UNIT:(2 * nblk + 1) * UNIT, :]

    @pl.when(t == 0)
    def _():
        kx_s[:, 0:UNIT, :] = jnp.zeros((ATTN_KV_HEADS, UNIT, LANES), BF16)
        vx_s[:, 0:UNIT, 0:LANES] = jnp.zeros((ATTN_KV_HEADS, UNIT, LANES), BF16)
        st_s[...] = jnp.zeros_like(st_s)
        for head in range(ATTN_KV_HEADS):
            for u in range(2 * nblk + 1):
                vx_s[head, u * UNIT:(u + 1) * UNIT, LANES:2 * LANES] = ones_ref[...]

    x = x_ref[0]
    scale = mod[:, D_MODEL:2 * D_MODEL]
    ms = jnp.mean(x * x, axis=-1, keepdims=True)
    h = (x * lax.rsqrt(ms + EPS)) * (normw_ref[...] * (1.0 + scale)) + mod[:, 0:D_MODEL]
    hb_s[...] = h.astype(BF16)

    def proj(c0, width=PROJ_CHUNK):
        return jnp.dot(hb_s[...], win_ref[:, c0:c0 + width], preferred_element_type=F32)

    def head_ssq(raw):
        return jnp.dot((raw * raw).astype(BF16), bd_ref[...], preferred_element_type=F32)

    lane = lax.broadcasted_iota(jnp.int32, (T, LANES), 1)
    first_half = (lane & (HEAD_DIM // 2)) == 0
    low_head = lane < HEAD_DIM
    row = lax.broadcasted_iota(jnp.int32, (T, LANES), 0)
    keep_prev = (row & (BLK - 1)) != 0

    def rope(v):
        partner = jnp.where(first_half, pltpu.roll(v, LANES - HEAD_DIM // 2, 1),
                            pltpu.roll(v, HEAD_DIM // 2, 1))
        return v * cos_s[...] + partner * sin_s[...]

    def rope_tables(r0, r1):
        fh = (lax.broadcasted_iota(jnp.int32, (r1 - r0, LANES), 1) & (HEAD_DIM // 2)) == 0
        trig = jnp.sin(pos_ref[0, r0:r1, :].astype(F32) * invf_ref[...] + phase_ref[...])
        cos_s[r0:r1, :] = jnp.where(fh, trig, pltpu.roll(trig, HEAD_DIM // 2, 1))
        sin_s[r0:r1, :] = jnp.where(fh, -pltpu.roll(trig, LANES - HEAD_DIM // 2, 1), trig)

    n_qp = ATTN_WIDTH // PROJ_CHUNK
    n_tab = 2 * n_qp
    for pc in range(n_qp):
        raw = proj(OFF_AQ + pc * PROJ_CHUNK)
        rq_s[:, pc * PROJ_CHUNK:(pc + 1) * PROJ_CHUNK] = raw
        rope_tables(2 * pc * T // n_tab, (2 * pc + 1) * T // n_tab)
        for sub in range(PROJ_CHUNK // CHUNK):
            cols = slice(pc * PROJ_CHUNK + sub * CHUNK, pc * PROJ_CHUNK + (sub + 1) * CHUNK)
            rf_s[:, cols] = head_ssq(raw[:, sub * CHUNK:(sub + 1) * CHUNK])
        rope_tables((2 * pc + 1) * T // n_tab, (2 * pc + 2) * T // n_tab)
    kv_raw = proj(OFF_AK, 2 * KV_WIDTH)
    k_raw = kv_raw[:, 0:KV_WIDTH]
    v_raw = kv_raw[:, KV_WIDTH:2 * KV_WIDTH]
    k_ssq = head_ssq(k_raw)

    def store_split(dst_s, pair, kv_pair):
        sw = pltpu.roll(pair, HEAD_DIM, 1)
        zero = jnp.zeros_like(pair)
        parts = ((2 * kv_pair, jnp.where(low_head, pair, zero), jnp.where(low_head, zero, sw)),
                 (2 * kv_pair + 1, jnp.where(low_head, sw, zero), jnp.where(low_head, zero, pair)))
        for head, lo, hi in parts:
            for role_prev in (False, True):
                lo_r = (jnp.where(keep_prev, lo, zero) if role_prev else lo).astype(BF16)
                hi_r = (jnp.where(keep_prev, hi, zero) if role_prev else hi).astype(BF16)
                for blk in range(nblk):
                    base = (2 * blk + (2 if role_prev else 1)) * UNIT
                    dst_s[head, base:base + BLK, 0:LANES] = lo_r[blk * BLK:(blk + 1) * BLK, :]
                    dst_s[head, base + BLK:base + UNIT, 0:LANES] = hi_r[blk * BLK:(blk + 1) * BLK, :]

    def q_epilogue(cb):
        cols = slice(cb * CHUNK, (cb + 1) * CHUNK)
        qn = rq_s[:, cols] * lax.rsqrt(rf_s[:, cols] * (1.0 / HEAD_DIM) + EPS) * qw_ref[...]
        for half in range(CHUNK // LANES):
            q_s[:, cb * CHUNK + half * LANES:cb * CHUNK + (half + 1) * LANES] = rope(
                qn[:, half * LANES:(half + 1) * LANES]).astype(BF16)

    def k_epilogue():
        kn = k_raw * lax.rsqrt(k_ssq * (1.0 / HEAD_DIM) + EPS) * kw_ref[...]
        for half in range(KV_WIDTH // LANES):
            store_split(kx_s, rope(kn[:, half * LANES:(half + 1) * LANES]), half)

    def v_epilogue():
        for half in range(KV_WIDTH // LANES):
            store_split(vx_s, v_raw[:, half * LANES:(half + 1) * LANES], half)

    side_work = [lambda: (q_epilogue(0), k_epilogue(), q_epilogue(1)),
                 lambda: (q_epilogue(2), v_epilogue(), q_epilogue(3))]
    for pc in range(ATTN_WIDTH // PROJ_CHUNK):
        cols = slice(pc * PROJ_CHUNK, (pc + 1) * PROJ_CHUNK)
        ag = proj(OFF_AG + pc * PROJ_CHUNK)
        side_work[pc]()
        ga_s[:, cols] = _silu(ag).astype(BF16)

    def att_scores(blk, j):
        rows = slice(blk * BLK, (blk + 1) * BLK)
        win_rows = slice(2 * blk * UNIT, (2 * blk + 2) * UNIT)
        p2 = jnp.concatenate(
            [q_s[rows, (2 * j) * LANES:(2 * j + 1) * LANES],
             q_s[rows, (2 * j + 1) * LANES:(2 * j + 2) * LANES]], axis=0)
        return lax.dot_general(p2, kx_s[j, win_rows, :], _NT, preferred_element_type=F32)

    def att_softmax_pv(blk, j, s2):
        win_rows = slice(2 * blk * UNIT, (2 * blk + 2) * UNIT)
        p_rows = []
        for r in range(2):
            p_prev = []
            p_cur = []
            for e in range(2):
                head = 4 * j + 2 * r + e
                prev_bias = sinkb_ref[0, head]
                if blk == 0:
                    prev_bias = jnp.where(t == 0, sinkb_ref[1, head], prev_bias)
                sp = s2[r * BLK:(r + 1) * BLK, e * BLK:(e + 1) * BLK] + prev_bias
                sc = s2[r * BLK:(r + 1) * BLK, (2 + e) * BLK:(3 + e) * BLK] + curb_ref[...]
                m = jnp.max(jnp.maximum(sp, sc), axis=1, keepdims=True)
                p_prev.append(jnp.exp2(sp - m).astype(BF16))
                p_cur.append(jnp.exp2(sc - m).astype(BF16))
            p_rows.append(jnp.concatenate(p_prev + p_cur, axis=1))
        pm = jnp.concatenate(p_rows, axis=0)
        return jnp.dot(pm, vx_s[j, win_rows, :], preferred_element_type=F32)

    def att_epilogue(blk, j, o2):
        rows = slice(blk * BLK, (blk + 1) * BLK)
        for r in range(2):
            cols = slice((2 * j + r) * LANES, (2 * j + r + 1) * LANES)
            num = o2[r * BLK:(r + 1) * BLK, 0:LANES]
            den = o2[r * BLK:(r + 1) * BLK, LANES:2 * LANES]
            mix_s[rows, cols] = (num * (1.0 / den) * ga_s[rows, cols].astype(F32)).astype(BF16)

    def rec_chunk(i):
        n = REC_WIDTH // PROJ_CHUNK
        which, pc = divmod(i, n)
        cols = slice(pc * PROJ_CHUNK, (pc + 1) * PROJ_CHUNK)
        if which == 0:
            rq_s[:, cols] = proj(OFF_RQ + pc * PROJ_CHUNK)
        elif which == 1:
            rf_s[:, cols] = proj(OFF_RF + pc * PROJ_CHUNK)
        elif which == 2:
            vr_s[:, cols] = proj(OFF_RI + pc * PROJ_CHUNK).astype(BF16)
        else:
            gr_s[:, cols] = _silu(proj(OFF_RG + pc * PROJ_CHUNK)).astype(BF16)

    units = [(blk, j) for blk in range(nblk) for j in range(ATTN_KV_HEADS)]
    n_rec_chunks = 4 * (REC_WIDTH // PROJ_CHUNK)

    row_i = lax.broadcasted_iota(jnp.int32, (BLK, BLK), 0)
    col_i = lax.broadcasted_iota(jnp.int32, (BLK, BLK), 1)
    causal = row_i >= col_i
    lb_row = lb_ref[...]

    def rec_front(blk):
        rows = slice(blk * BLK, (blk + 1) * BLK)
        f = lb_row + (1.0 - lb_row) * _sigmoid(rf_s[rows, :])
        lf = jnp.log2(f)
        kk = 1.0 - f
        lf_hi = lf.astype(BF16)
        lf_lo = (lf - lf_hi.astype(F32)).astype(BF16)
        brel = jnp.dot(mrel_ref[...], jnp.concatenate([lf_hi, lf_lo], axis=0),
                       preferred_element_type=F32)
        epos = jnp.exp2(brel)
        eneg = jnp.exp2(-brel)
        d_mid = f[0:1, :] * eneg[0:1, :]
        d_tail = epos[BLK - 1:BLK, :]
        q_rel = rq_s[rows, :] * epos
        q_abs = (q_rel * d_mid).astype(BF16)
        q_rel = q_rel.astype(BF16)
        k_inv = kk * eneg
        k_out = (k_inv * d_tail).astype(BF16)
        k_inv = k_inv.astype(BF16)
        d_all = d_tail * d_mid

        a_mats = []
        vts = []
        us = []
        for hh in range(REC_HEADS):
            cols = slice(hh * REC_DIM, (hh + 1) * REC_DIM)
            a_mats.append(lax.dot_general(q_rel[:, cols], k_inv[:, cols], _NT,
                                          preferred_element_type=F32))
            vt = vr_s[rows, cols].T
            vts.append(vt)
            us.append(jnp.dot(vt, k_out[:, cols], preferred_element_type=F32))
        return a_mats, q_abs, vts, us, d_all

    def rec_back(blk, a_mats, q_abs, vts, us, d_all):
        rows = slice(blk * BLK, (blk + 1) * BLK)
        outs = []
        for hh in range(REC_HEADS):
            cols = slice(hh * REC_DIM, (hh + 1) * REC_DIM)
            a = jnp.where(causal, a_mats[hh], 0.0).astype(BF16)
            st = st_s[hh]
            outs.append(lax.dot_general(jnp.concatenate([a, q_abs[:, cols]], axis=1),
                                        jnp.concatenate([vts[hh], st.astype(BF16)], axis=1), _NT,
                                        preferred_element_type=F32))
            st_s[hh] = st * d_all[:, cols] + us[hh]

        for hh in range(REC_HEADS):
            cols = slice(hh * REC_DIM, (hh + 1) * REC_DIM)
            o = outs[hh]
            rr = lax.rsqrt(jnp.mean(o * o, axis=1, keepdims=True) + EPS)
            rec = (o * rr) * recw_ref[...] * gr_s[rows, cols].astype(F32)
            mix_s[rows, ATTN_WIDTH + hh * REC_DIM:ATTN_WIDTH + (hh + 1) * REC_DIM] = rec.astype(BF16)

    first_front = len(units) - nblk - 1
    assert first_front >= 2 * (3 * n_rec_chunks // 4 - 1) + 1
    fronts = {}
    scores = att_scores(*units[0])
    pending = None
    for u, (blk, j) in enumerate(units):
        for i in _share(u, n_rec_chunks, len(units)):
            rec_chunk(i)
        if first_front <= u < first_front + nblk:
            fronts[u - first_front] = rec_front(u - first_front)
        nxt = att_scores(*units[u + 1]) if u + 1 < len(units) else None
        o2 = att_softmax_pv(blk, j, scores)
        if pending is not None:
            att_epilogue(*pending)
        pending = (blk, j, o2)
        scores = nxt
    att_epilogue(*pending)

    n_out = D_MODEL // PROJ_CHUNK
    for blk in range(nblk):
        for pc in _share(blk, n_out, nblk):
            oc = slice(pc * PROJ_CHUNK, (pc + 1) * PROJ_CHUNK)
            ya_s[:, oc] = jnp.dot(mix_s[:, 0:ATTN_WIDTH], wout_ref[0:ATTN_WIDTH, oc],
                                  preferred_element_type=F32)
        rec_back(blk, *fronts.pop(blk))

    y = ya_s[...] + jnp.dot(mix_s[:, ATTN_WIDTH:ATTN_WIDTH + REC_WIDTH],
                            wout_ref[ATTN_WIDTH:ATTN_WIDTH + REC_WIDTH, :],
                            preferred_element_type=F32)
    o_ref[0] = x_ref[0] + mod[:, 2 * D_MODEL:3 * D_MODEL] * y


def _constants():
    head_of = np.arange(CHUNK) // HEAD_DIM
    bd = (head_of[:, None] == head_of[None, :]).astype(np.float32)
    mid = BLK // 2 - 1
    tt = np.arange(BLK)[:, None]
    rr = np.arange(BLK)[None, :]
    mrel = ((rr <= tt).astype(np.float32) - (rr <= mid).astype(np.float32))
    mrel2 = np.concatenate([mrel, mrel], axis=1)
    assert WINDOW == BLK
    qi = np.arange(BLK)[:, None]
    kj = np.arange(BLK)[None, :]
    ninf = np.float32(-np.inf)
    prev = np.where(kj > qi, 0.0, ninf).astype(np.float32)
    cur = np.where(kj <= qi, 0.0, ninf).astype(np.float32)
    none = np.full((BLK, BLK), ninf, np.float32)
    lane = np.arange(LANES)
    phase = np.where((lane % HEAD_DIM) < HEAD_DIM // 2, np.pi / 2, 0.0).astype(np.float32)[None, :]
    ones = np.zeros((UNIT, LANES), np.float32)
    ones[0:BLK, 0:HEAD_DIM] = 1.0
    ones[BLK:UNIT, HEAD_DIM:LANES] = 1.0
    return bd, mrel2, np.stack([prev, none]), cur, phase, ones


def kernel(x, c, positions, norm_w, w_ada, b_ada, w_in, q_norm_w, k_norm_w, sinks, rec_norm_w,
           lower_bounds, w_out):
    B, S, D = x.shape
    assert D == D_MODEL and w_in.shape == (1, D_MODEL, IN_WIDTH) and S % TOKEN_TILE == 0
    assert lower_bounds.shape[0] == 2 and w_out.shape == (1, ATTN_WIDTH + REC_WIDTH, D_MODEL)
    T = TOKEN_TILE

    n_ada = 3
    mod, lb = pl.pallas_call(
        _ada_kernel,
        grid=(n_ada,),
        in_specs=[
            pl.BlockSpec((B, D), lambda i: (0, 0)),
            pl.BlockSpec((None, D, D), lambda i: (0, 0, i)),
            pl.BlockSpec((1, D), lambda i: (0, i)),
            pl.BlockSpec((2, REC_WIDTH), lambda i: (0, 0)),
        ],
        out_specs=[
            pl.BlockSpec((B, D), lambda i: (0, i)),
            pl.BlockSpec((1, REC_WIDTH), lambda i: (0, 0)),
        ],
        out_shape=[jax.ShapeDtypeStruct((B, 3 * D), F32),
                   jax.ShapeDtypeStruct((1, REC_WIDTH), F32)],
        compiler_params=pltpu.CompilerParams(dimension_semantics=("arbitrary",)),
        name="ada_modulation",
    )(c, w_ada, b_ada, lower_bounds)

    bd, mrel2, prev_bias, cur_bias, phase, ones = _constants()
    inv_freq = ROPE_THETA ** (-jnp.arange(0, HEAD_DIM, 2, dtype=F32) / HEAD_DIM)
    invf = jnp.tile(inv_freq, LANES // (HEAD_DIM // 2))[None, :]
    log2e = math.log2(math.e)
    qw = jnp.tile(q_norm_w[0], CHUNK // HEAD_DIM)[None, :] * (HEAD_DIM ** -0.5 * log2e)
    kw = jnp.tile(k_norm_w[0], CHUNK // HEAD_DIM)[None, :]
    sink_bias = jnp.where(jnp.arange(BLK)[None, None, None, :] == 0,
                          (sinks[0].astype(F32) * log2e)[None, :, None, None],
                          jnp.asarray(prev_bias)[:, None, :, :])

    const = lambda shape: pl.BlockSpec(shape, lambda b, t: (0,) * len(shape))
    resident = lambda shape: pl.BlockSpec(shape, lambda b, t: (0,) * len(shape),
                                          pipeline_mode=pl.Buffered(1))
    kv_rows = (2 * (T // BLK) + 1) * UNIT
    out = pl.pallas_call(
        _layer_kernel,
        grid=(B, S // T),
        in_specs=[
            pl.BlockSpec((1, T, D), lambda b, t: (b, t, 0)),
            pl.BlockSpec((1, T, 1), lambda b, t: (b, t, 0)),
            pl.BlockSpec((1, 1, 3 * D), lambda b, t: (b, 0, 0)),
            const((1, D)),
            resident((D, IN_WIDTH)),
            resident((ATTN_WIDTH + REC_WIDTH, D)),
            const((1, CHUNK)), const((1, CHUNK)), const((1, REC_DIM)),
            const((1, REC_WIDTH)),
            const((1, LANES)), const((1, LANES)),
            const((CHUNK, CHUNK)), const((BLK, 2 * BLK)),
            resident((2, ATTN_HEADS, BLK, BLK)),
            const((BLK, BLK)),
            const((UNIT, LANES)),
        ],
        out_specs=pl.BlockSpec((1, T, D), lambda b, t: (b, t, 0)),
        out_shape=jax.ShapeDtypeStruct((B, S, D), x.dtype),
        scratch_shapes=[
            pltpu.VMEM((T, D), BF16),
            pltpu.VMEM((T, LANES), F32),
            pltpu.VMEM((T, LANES), F32),
            pltpu.VMEM((T, ATTN_WIDTH), BF16),
            pltpu.VMEM((ATTN_KV_HEADS, kv_rows, LANES), BF16),
            pltpu.VMEM((ATTN_KV_HEADS, kv_rows, 2 * LANES), BF16),
            pltpu.VMEM((T, ATTN_WIDTH), BF16),
            pltpu.VMEM((T, REC_WIDTH), F32),
            pltpu.VMEM((T, REC_WIDTH), F32),
            pltpu.VMEM((T, REC_WIDTH), BF16),
            pltpu.VMEM((T, REC_WIDTH), BF16),
            pltpu.VMEM((REC_HEADS, REC_DIM, REC_DIM), F32),
            pltpu.VMEM((T, ATTN_WIDTH + REC_WIDTH), BF16),
            pltpu.VMEM((T, D), F32),
        ],
        compiler_params=pltpu.CompilerParams(
            dimension_semantics=("arbitrary", "arbitrary"),
            vmem_limit_bytes=VMEM_LIMIT_BYTES),
        name="hybrid_layer",
    )(x, positions[:, :, None], mod[:, None, :], norm_w, w_in[0].astype(BF16),
      w_out[0].astype(BF16), qw, kw, rec_norm_w, lb, invf, jnp.asarray(phase),
      jnp.asarray(bd, BF16), jnp.asarray(mrel2, BF16), sink_bias, jnp.asarray(cur_bias),
      jnp.asarray(ones, BF16))
    return out
```

```python
import math

import jax
import jax.numpy as jnp
import numpy as np
from jax import lax
from jax.experimental import pallas as pl
from jax.experimental.pallas import tpu as pltpu

F32 = jnp.float32
BF16 = jnp.bfloat16

D_MODEL = 1024
ATTN_HEADS = 16
ATTN_KV_HEADS = 4
HEAD_DIM = 64
KV_WIDTH = ATTN_KV_HEADS * HEAD_DIM
ATTN_WIDTH = ATTN_HEADS * HEAD_DIM
WINDOW = 128
ROPE_THETA = 10000.0
REC_HEADS = 8
REC_DIM = 128
REC_WIDTH = REC_HEADS * REC_DIM
EPS = 1e-6
BLK = 128
LANES = 128
UNIT = 2 * BLK
CHUNK = 256
PROJ_CHUNK = 512

OFF_AQ = 0
OFF_AK = OFF_AQ + ATTN_WIDTH
OFF_AV = OFF_AK + KV_WIDTH
OFF_AG = OFF_AV + KV_WIDTH
OFF_RQ = OFF_AG + ATTN_WIDTH
OFF_RF = OFF_RQ + REC_WIDTH
OFF_RI = OFF_RF + REC_WIDTH
OFF_RG = OFF_RI + REC_WIDTH
IN_WIDTH = OFF_RG + REC_WIDTH

TOKEN_TILE = 512
VMEM_LIMIT_BYTES = 58 * 1024 * 1024

_NT = (((1,), (1,)), ((), ()))


def _share(slot, n_items, n_slots):
    return range(-(-slot * n_items // n_slots), -(-(slot + 1) * n_items // n_slots))


def _silu(x):
    hx = 0.5 * x
    return hx * jnp.tanh(hx) + hx


def _ada_kernel(c_ref, w_ref, b_ref, lbin_ref, mod_ref, lb_ref):
    mod_ref[...] = jnp.dot(_silu(c_ref[...]).astype(BF16), w_ref[...].astype(BF16),
                           preferred_element_type=F32) + b_ref[...]
    lbv = lbin_ref[...]
    e = jnp.exp(lbv - jnp.max(lbv, axis=0, keepdims=True))
    lb_ref[...] = e[0:1, :] / jnp.sum(e, axis=0, keepdims=True)


def _layer_kernel(x_ref, pos_ref, mod_ref, normw_ref, win_ref, wout_ref,
                  qw_ref, kw_ref, recw_ref, lb_ref, invf_ref, phase_ref, bd_ref, mrel_ref,
                  sinkb_ref, curb_ref, ones_ref, o_ref,
                  hb_s, cos_s, sin_s, q_s, kx_s, vx_s, ga_s,
                  rq_s, rf_s, vr_s, gr_s, st_s, mix_s, ya_s):
    T = x_ref.shape[1]
    nblk = T // BLK
    t = pl.program_id(1)
    mod = mod_ref[0]

    @pl.when(t > 0)
    def _():
        kx_s[:, 0:UNIT, :] = kx_s[:, 2 * nblk * UNIT:(2 * nblk + 1) * UNIT, :]
        vx_s[:, 0:UNIT, :] = vx_s[:, 2 * nblk * UNIT:(2 * nblk + 1) * UNIT, :]

    @pl.when(t == 0)
    def _():
        kx_s[:, 0:UNIT, :] = jnp.zeros((ATTN_KV_HEADS, UNIT, LANES), BF16)
        vx_s[:, 0:UNIT, 0:LANES] = jnp.zeros((ATTN_KV_HEADS, UNIT, LANES), BF16)
        st_s[...] = jnp.zeros_like(st_s)
        for head in range(ATTN_KV_HEADS):
            for u in range(2 * nblk + 1):
                vx_s[head, u * UNIT:(u + 1) * UNIT, LANES:2 * LANES] = ones_ref[...]

    x = x_ref[0]
    scale = mod[:, D_MODEL:2 * D_MODEL]
    ms = jnp.mean(x * x, axis=-1, keepdims=True)
    h = (x * lax.rsqrt(ms + EPS)) * (normw_ref[...] * (1.0 + scale)) + mod[:, 0:D_MODEL]
    hb_s[...] = h.astype(BF16)

    def proj(c0, width=PROJ_CHUNK):
        return jnp.dot(hb_s[...], win_ref[:, c0:c0 + width], preferred_element_type=F32)

    def head_ssq(raw):
        return jnp.dot((raw * raw).astype(BF16), bd_ref[...], preferred_element_type=F32)

    lane = lax.broadcasted_iota(jnp.int32, (T, LANES), 1)
    first_half = (lane & (HEAD_DIM // 2)) == 0
    low_head = lane < HEAD_DIM
    row = lax.broadcasted_iota(jnp.int32, (T, LANES), 0)
    keep_prev = (row & (BLK - 1)) != 0

    def rope(v):
        partner = jnp.where(first_half, pltpu.roll(v, LANES - HEAD_DIM // 2, 1),
                            pltpu.roll(v, HEAD_DIM // 2, 1))
        return v * cos_s[...] + partner * sin_s[...]

    def rope_tables(r0, r1):
        fh = (lax.broadcasted_iota(jnp.int32, (r1 - r0, LANES), 1) & (HEAD_DIM // 2)) == 0
        trig = jnp.sin(pos_ref[0, r0:r1, :].astype(F32) * invf_ref[...] + phase_ref[...])
        cos_s[r0:r1, :] = jnp.where(fh, trig, pltpu.roll(trig, HEAD_DIM // 2, 1))
        sin_s[r0:r1, :] = jnp.where(fh, -pltpu.roll(trig, LANES - HEAD_DIM // 2, 1), trig)

    n_qp = ATTN_WIDTH // PROJ_CHUNK
    n_tab = 2 * n_qp
    for pc in range(n_qp):
        raw = proj(OFF_AQ + pc * PROJ_CHUNK)
        rq_s[:, pc * PROJ_CHUNK:(pc + 1) * PROJ_CHUNK] = raw
        rope_tables(2 * pc * T // n_tab, (2 * pc + 1) * T // n_tab)
        for sub in range(PROJ_CHUNK // CHUNK):
            cols = slice(pc * PROJ_CHUNK + sub * CHUNK, pc * PROJ_CHUNK + (sub + 1) * CHUNK)
            rf_s[:, cols] = head_ssq(raw[:, sub * CHUNK:(sub + 1) * CHUNK])
        rope_tables((2 * pc + 1) * T // n_tab, (2 * pc + 2) * T // n_tab)
    kv_raw = proj(OFF_AK, 2 * KV_WIDTH)
    k_raw = kv_raw[:, 0:KV_WIDTH]
    v_raw = kv_raw[:, KV_WIDTH:2 * KV_WIDTH]
    k_ssq = head_ssq(k_raw)

    def store_split(dst_s, pair, kv_pair):
        sw = pltpu.roll(pair, HEAD_DIM, 1)
        zero = jnp.zeros_like(pair)
        parts = ((2 * kv_pair, jnp.where(low_head, pair, zero), jnp.where(low_head, zero, sw)),
                 (2 * kv_pair + 1, jnp.where(low_head, sw, zero), jnp.where(low_head, zero, pair)))
        for head, lo, hi in parts:
            for role_prev in (False, True):
                lo_r = (jnp.where(keep_prev, lo, zero) if role_prev else lo).astype(BF16)
                hi_r = (jnp.where(keep_prev, hi, zero) if role_prev else hi).astype(BF16)
                for blk in range(nblk):
                    base = (2 * blk + (2 if role_prev else 1)) * UNIT
                    dst_s[head, base:base + BLK, 0:LANES] = lo_r[blk * BLK:(blk + 1) * BLK, :]
                    dst_s[head, base + BLK:base + UNIT, 0:LANES] = hi_r[blk * BLK:(blk + 1) * BLK, :]

    def q_epilogue(cb):
        cols = slice(cb * CHUNK, (cb + 1) * CHUNK)
        qn = rq_s[:, cols] * lax.rsqrt(rf_s[:, cols] * (1.0 / HEAD_DIM) + EPS) * qw_ref[...]
        for half in range(CHUNK // LANES):
            q_s[:, cb * CHUNK + half * LANES:cb * CHUNK + (half + 1) * LANES] = rope(
                qn[:, half * LANES:(half + 1) * LANES]).astype(BF16)

    def k_epilogue():
        kn = k_raw * lax.rsqrt(k_ssq * (1.0 / HEAD_DIM) + EPS) * kw_ref[...]
        for half in range(KV_WIDTH // LANES):
            store_split(kx_s, rope(kn[:, half * LANES:(half + 1) * LANES]), half)

    def v_epilogue():
        for half in range(KV_WIDTH // LANES):
            store_split(vx_s, v_raw[:, half * LANES:(half + 1) * LANES], half)

    side_work = [lambda: (q_epilogue(0), k_epilogue(), q_epilogue(1)),
                 lambda: (q_epilogue(2), v_epilogue(), q_epilogue(3))]
    for pc in range(ATTN_WIDTH // PROJ_CHUNK):
        cols = slice(pc * PROJ_CHUNK, (pc + 1) * PROJ_CHUNK)
        ag = proj(OFF_AG + pc * PROJ_CHUNK)
        side_work[pc]()
        ga_s[:, cols] = _silu(ag).astype(BF16)

    def att_scores(blk, j):
        rows = slice(blk * BLK, (blk + 1) * BLK)
        win_rows = slice(2 * blk * UNIT, (2 * blk + 2) * UNIT)
        p2 = jnp.concatenate(
            [q_s[rows, (2 * j) * LANES:(2 * j + 1) * LANES],
             q_s[rows, (2 * j + 1) * LANES:(2 * j + 2) * LANES]], axis=0)
        return lax.dot_general(p2, kx_s[j, win_rows, :], _NT, preferred_element_type=F32)

    def att_softmax_pv(blk, j, s2):
        win_rows = slice(2 * blk * UNIT, (2 * blk + 2) * UNIT)
        p_rows = []
        for r in range(2):
            p_prev = []
            p_cur = []
            for e in range(2):
                head = 4 * j + 2 * r + e
                prev_bias = sinkb_ref[0, head]
                if blk == 0:
                    prev_bias = jnp.where(t == 0, sinkb_ref[1, head], prev_bias)
                sp = s2[r * BLK:(r + 1) * BLK, e * BLK:(e + 1) * BLK] + prev_bias
                sc = s2[r * BLK:(r + 1) * BLK, (2 + e) * BLK:(3 + e) * BLK] + curb_ref[...]
                m = jnp.max(jnp.maximum(sp, sc), axis=1, keepdims=True)
                p_prev.append(jnp.exp2(sp - m).astype(BF16))
                p_cur.append(jnp.exp2(sc - m).astype(BF16))
            p_rows.append(jnp.concatenate(p_prev + p_cur, axis=1))
        pm = jnp.concatenate(p_rows, axis=0)
        return jnp.dot(pm, vx_s[j, win_rows, :], preferred_element_type=F32)

    def att_epilogue(blk, j, o2):
        rows = slice(blk * BLK, (blk + 1) * BLK)
        for r in range(2):
            cols = slice((2 * j + r) * LANES, (2 * j + r + 1) * LANES)
            num = o2[r * BLK:(r + 1) * BLK, 0:LANES]
            den = o2[r * BLK:(r + 1) * BLK, LANES:2 * LANES]
            mix_s[rows, cols] = (num * (1.0 / den) * ga_s[rows, cols].astype(F32)).astype(BF16)

    def rec_chunk(i):
        n = REC_WIDTH // PROJ_CHUNK
        which, pc = divmod(i, n)
        cols = slice(pc * PROJ_CHUNK, (pc + 1) * PROJ_CHUNK)
        if which == 0:
            rq_s[:, cols] = proj(OFF_RQ + pc * PROJ_CHUNK)
        elif which == 1:
            rf_s[:, cols] = proj(OFF_RF + pc * PROJ_CHUNK)
        elif which == 2:
            vr_s[:, cols] = proj(OFF_RI + pc * PROJ_CHUNK).astype(BF16)
        else:
            gr_s[:, cols] = _silu(proj(OFF_RG + pc * PROJ_CHUNK)).astype(BF16)

    units = [(blk, j) for blk in range(nblk) for j in range(ATTN_KV_HEADS)]
    n_rec_chunks = 4 * (REC_WIDTH // PROJ_CHUNK)

    row_i = lax.broadcasted_iota(jnp.int32, (BLK, BLK), 0)
    col_i = lax.broadcasted_iota(jnp.int32, (BLK, BLK), 1)
    causal = row_i >= col_i
    lb_row = lb_ref[...]

    def rec_front(blk):
        rows = slice(blk * BLK, (blk + 1) * BLK)
        f = (0.5 + 0.5 * lb_row) + (0.5 - 0.5 * lb_row) * jnp.tanh(0.5 * rf_s[rows, :])
        lf = jnp.log2(f)
        kk = 1.0 - f
        lf_hi = lf.astype(BF16)
        lf_lo = (lf - lf_hi.astype(F32)).astype(BF16)
        brel = jnp.dot(mrel_ref[...], jnp.concatenate([lf_hi, lf_lo], axis=0),
                       preferred_element_type=F32)
        epos = jnp.exp2(brel)
        eneg = jnp.exp2(-brel)
        d_mid = f[0:1, :] * eneg[0:1, :]
        d_tail = epos[BLK - 1:BLK, :]
        q_rel = rq_s[rows, :] * epos
        q_abs = (q_rel * d_mid).astype(BF16)
        q_rel = q_rel.astype(BF16)
        k_inv = kk * eneg
        k_out = (k_inv * d_tail).astype(BF16)
        k_inv = k_inv.astype(BF16)
        d_all = d_tail * d_mid

        per_head = []
        for hh in range(REC_HEADS):
            cols = slice(hh * REC_DIM, (hh + 1) * REC_DIM)
            a = lax.dot_general(q_rel[:, cols], k_inv[:, cols], _NT, preferred_element_type=F32)
            vt = vr_s[rows, cols].T
            u_mat = jnp.dot(vt, k_out[:, cols], preferred_element_type=F32)
            per_head.append((a, q_abs[:, cols], vt, u_mat, d_all[:, cols]))
        return per_head

    def rec_back(blk, per_head):
        rows = slice(blk * BLK, (blk + 1) * BLK)
        outs = []
        for hh, (a, q_abs, vt, u_mat, d_all) in enumerate(per_head):
            a = jnp.where(causal, a, 0.0).astype(BF16)
            st = st_s[hh]
            outs.append(lax.dot_general(jnp.concatenate([a, q_abs], axis=1),
                                        jnp.concatenate([vt, st.astype(BF16)], axis=1), _NT,
                                        preferred_element_type=F32))
            st_s[hh] = st * d_all + u_mat

        for hh in range(REC_HEADS):
            cols = slice(hh * REC_DIM, (hh + 1) * REC_DIM)
            o = outs[hh]
            rr = lax.rsqrt(jnp.mean(o * o, axis=1, keepdims=True) + EPS)
            rec = (o * rr) * recw_ref[...] * gr_s[rows, cols].astype(F32)
            mix_s[rows, ATTN_WIDTH + hh * REC_DIM:ATTN_WIDTH + (hh + 1) * REC_DIM] = rec.astype(BF16)

    first_front = len(units) - nblk - 1
    assert first_front >= 2 * (3 * n_rec_chunks // 4 - 1) + 1
    fronts = {}
    scores = att_scores(*units[0])
    pending = None
    for u, (blk, j) in enumerate(units):
        for i in _share(u, n_rec_chunks, len(units)):
            rec_chunk(i)
        if first_front <= u < first_front + nblk:
            fronts[u - first_front] = rec_front(u - first_front)
        nxt = att_scores(*units[u + 1]) if u + 1 < len(units) else None
        o2 = att_softmax_pv(blk, j, scores)
        if pending is not None:
            att_epilogue(*pending)
        pending = (blk, j, o2)
        scores = nxt
    att_epilogue(*pending)

    n_out = D_MODEL // PROJ_CHUNK
    for blk in range(nblk):
        for pc in _share(blk, n_out, nblk):
            oc = slice(pc * PROJ_CHUNK, (pc + 1) * PROJ_CHUNK)
            ya_s[:, oc] = jnp.dot(mix_s[:, 0:ATTN_WIDTH], wout_ref[0:ATTN_WIDTH, oc],
                                  preferred_element_type=F32)
        rec_back(blk, fronts.pop(blk))

    y = ya_s[...] + jnp.dot(mix_s[:, ATTN_WIDTH:ATTN_WIDTH + REC_WIDTH],
                            wout_ref[ATTN_WIDTH:ATTN_WIDTH + REC_WIDTH, :],
                            preferred_element_type=F32)
    o_ref[0] = x_ref[0] + mod[:, 2 * D_MODEL:3 * D_MODEL] * y


def _constants():
    head_of = np.arange(CHUNK) // HEAD_DIM
    bd = (head_of[:, None] == head_of[None, :]).astype(np.float32)
    mid = BLK // 2 - 1
    tt = np.arange(BLK)[:, None]
    rr = np.arange(BLK)[None, :]
    mrel = ((rr <= tt).astype(np.float32) - (rr <= mid).astype(np.float32))
    mrel2 = np.concatenate([mrel, mrel], axis=1)
    assert WINDOW == BLK
    qi = np.arange(BLK)[:, None]
    kj = np.arange(BLK)[None, :]
    ninf = np.float32(-np.inf)
    prev = np.where(kj > qi, 0.0, ninf).astype(np.float32)
    cur = np.where(kj <= qi, 0.0, ninf).astype(np.float32)
    none = np.full((BLK, BLK), ninf, np.float32)
    lane = np.arange(LANES)
    phase = np.where((lane % HEAD_DIM) < HEAD_DIM // 2, np.pi / 2, 0.0).astype(np.float32)[None, :]
    ones = np.zeros((UNIT, LANES), np.float32)
    ones[0:BLK, 0:HEAD_DIM] = 1.0
    ones[BLK:UNIT, HEAD_DIM:LANES] = 1.0
    return bd, mrel2, np.stack([prev, none]), cur, phase, ones


def kernel(x, c, positions, norm_w, w_ada, b_ada, w_in, q_norm_w, k_norm_w, sinks, rec_norm_w,
           lower_bounds, w_out):
    B, S, D = x.shape
    assert D == D_MODEL and w_in.shape == (1, D_MODEL, IN_WIDTH) and S % TOKEN_TILE == 0
    assert lower_bounds.shape[0] == 2 and w_out.shape == (1, ATTN_WIDTH + REC_WIDTH, D_MODEL)
    T = TOKEN_TILE

    n_ada = 3
    mod, lb = pl.pallas_call(
        _ada_kernel,
        grid=(n_ada,),
        in_specs=[
            pl.BlockSpec((B, D), lambda i: (0, 0)),
            pl.BlockSpec((None, D, D), lambda i: (0, 0, i)),
            pl.BlockSpec((1, D), lambda i: (0, i)),
            pl.BlockSpec((2, REC_WIDTH), lambda i: (0, 0)),
        ],
        out_specs=[
            pl.BlockSpec((B, D), lambda i: (0, i)),
            pl.BlockSpec((1, REC_WIDTH), lambda i: (0, 0)),
        ],
        out_shape=[jax.ShapeDtypeStruct((B, 3 * D), F32),
                   jax.ShapeDtypeStruct((1, REC_WIDTH), F32)],
        compiler_params=pltpu.CompilerParams(dimension_semantics=("arbitrary",)),
        name="ada_modulation",
    )(c, w_ada, b_ada, lower_bounds)

    bd, mrel2, prev_bias, cur_bias, phase, ones = _constants()
    inv_freq = ROPE_THETA ** (-jnp.arange(0, HEAD_DIM, 2, dtype=F32) / HEAD_DIM)
    invf = jnp.tile(inv_freq, LANES // (HEAD_DIM // 2))[None, :]
    log2e = math.log2(math.e)
    qw = jnp.tile(q_norm_w[0], CHUNK // HEAD_DIM)[None, :] * (HEAD_DIM ** -0.5 * log2e)
    kw = jnp.tile(k_norm_w[0], CHUNK // HEAD_DIM)[None, :]
    sink_bias = jnp.where(jnp.arange(BLK)[None, None, None, :] == 0,
                          (sinks[0].astype(F32) * log2e)[None, :, None, None],
                          jnp.asarray(prev_bias)[:, None, :, :])

    const = lambda shape: pl.BlockSpec(shape, lambda b, t: (0,) * len(shape))
    resident = lambda shape: pl.BlockSpec(shape, lambda b, t: (0,) * len(shape),
                                          pipeline_mode=pl.Buffered(1))
    kv_rows = (2 * (T // BLK) + 1) * UNIT
    out = pl.pallas_call(
        _layer_kernel,
        grid=(B, S // T),
        in_specs=[
            pl.BlockSpec((1, T, D), lambda b, t: (b, t, 0)),
            pl.BlockSpec((1, T, LANES), lambda b, t: (b, t, 0)),
            pl.BlockSpec((1, 1, 3 * D), lambda b, t: (b, 0, 0)),
            const((1, D)),
            resident((D, IN_WIDTH)),
            resident((ATTN_WIDTH + REC_WIDTH, D)),
            const((1, CHUNK)), const((1, CHUNK)), const((1, REC_DIM)),
            const((1, REC_WIDTH)),
            const((1, LANES)), const((1, LANES)),
            const((CHUNK, CHUNK)), const((BLK, 2 * BLK)),
            resident((2, ATTN_HEADS, BLK, BLK)),
            const((BLK, BLK)),
            const((UNIT, LANES)),
        ],
        out_specs=pl.BlockSpec((1, T, D), lambda b, t: (b, t, 0)),
        out_shape=jax.ShapeDtypeStruct((B, S, D), x.dtype),
        scratch_shapes=[
            pltpu.VMEM((T, D), BF16),
            pltpu.VMEM((T, LANES), F32),
            pltpu.VMEM((T, LANES), F32),
            pltpu.VMEM((T, ATTN_WIDTH), BF16),
            pltpu.VMEM((ATTN_KV_HEADS, kv_rows, LANES), BF16),
            pltpu.VMEM((ATTN_KV_HEADS, kv_rows, 2 * LANES), BF16),
            pltpu.VMEM((T, ATTN_WIDTH), BF16),
            pltpu.VMEM((T, REC_WIDTH), F32),
            pltpu.VMEM((T, REC_WIDTH), F32),
            pltpu.VMEM((T, REC_WIDTH), BF16),
            pltpu.VMEM((T, REC_WIDTH), BF16),
            pltpu.VMEM((REC_HEADS, REC_DIM, REC_DIM), F32),
            pltpu.VMEM((T, ATTN_WIDTH + REC_WIDTH), BF16),
            pltpu.VMEM((T, D), F32),
        ],
        compiler_params=pltpu.CompilerParams(
            dimension_semantics=("arbitrary", "arbitrary"),
            vmem_limit_bytes=VMEM_LIMIT_BYTES),
        name="hybrid_layer",
    )(x, jnp.broadcast_to(positions[:, :, None], (B, S, LANES)), mod[:, None, :], norm_w,
      w_in[0].astype(BF16),
      w_out[0].astype(BF16), qw, kw, rec_norm_w, lb, invf, jnp.asarray(phase),
      jnp.asarray(bd, BF16), jnp.asarray(mrel2, BF16), sink_bias, jnp.asarray(cur_bias),
      jnp.asarray(ones, BF16))
    return out
```
